```python
import math
import jax, jax.numpy as jnp
from jax import lax
import numpy as np

D_MODEL = 2048
BATCH = 2
SEQ = 4096
DEPTH = 4
DEC_BATCH = 32
DEC_SEQ = 4
PAST_LEN = 16384
PAGE_SIZE = 128

HD_A = 128
H_A = D_MODEL // HD_A
KV_A = 4
G_A = H_A // KV_A
QW_A = H_A * HD_A
H_IDX = 16
HD_IDX = 64
TOPK_MAX = 256
IDX_W_SCALE = (H_IDX * HD_IDX) ** -0.5
Q_BLOCK = 128
HD_B = 64
H_B = D_MODEL // HD_B
KV_B = 4
G_B = H_B // KV_B
QW_B = H_B * HD_B
KVW_B = 2 * KV_B * HD_B
WINDOW = 128
ROPE_THETA = 500000.0
ROT_A = HD_A // 4
ROT_IDX = HD_IDX // 4
ROT_B = HD_B // 4
D_FF = ((8 * D_MODEL // 3 + 255) // 256) * 256
N_EXPERTS = 8
TOP_K_EXPERTS = 2
D_FF_EXPERT = 7 * D_MODEL // 2
EPS = 1e-6
N_A_LAYERS = DEPTH // 2
N_B_LAYERS = DEPTH - N_A_LAYERS
N_DENSE = (DEPTH + 1) // 2
N_MOE = DEPTH // 2
IN_A = QW_A + 2 * KV_A * HD_A + H_IDX * HD_IDX + HD_IDX + H_IDX

kernel_name = 'yoco_dsa_swa_sink_decoder_step'


def rms_norm(x, g):
    xf = x.astype(jnp.float32)
    y = xf * lax.rsqrt(jnp.mean(xf * xf, axis=-1, keepdims=True) + EPS)
    return (y * g.astype(jnp.float32)).astype(x.dtype)


def adaln(x, g, shift, scale):
    return rms_norm(x, g) * (1 + scale[:, None, :]) + shift[:, None, :]


def rope(x, pos, rot):
    half = rot // 2
    inv = jnp.exp(-math.log(ROPE_THETA) * jnp.arange(half, dtype=jnp.float32) / half)
    ang = pos.astype(jnp.float32)[:, None] * inv[None, :]
    cos = jnp.cos(ang)[:, None, :]
    sin = jnp.sin(ang)[:, None, :]
    xf = x.astype(jnp.float32)
    x1 = xf[..., :half]
    x2 = xf[..., half:rot]
    out = jnp.concatenate([x1 * cos - x2 * sin, x2 * cos + x1 * sin, xf[..., rot:]], axis=-1)
    return out.astype(x.dtype)


def swiglu(h, wg, wu, wd):
    return (jax.nn.silu(h @ wg) * (h @ wu)) @ wd


def moe_ffn(h, w_r, b_r, wg, wu, wd):
    logits = (h @ w_r).astype(jnp.float32) + b_r.astype(jnp.float32)
    vals, idx = lax.top_k(logits, TOP_K_EXPERTS)
    gates = jax.nn.softmax(vals, axis=-1)
    combine = jnp.einsum('nk,nke->ne', gates, jax.nn.one_hot(idx, N_EXPERTS, dtype=jnp.float32)).astype(h.dtype)
    out = jnp.zeros_like(h)
    for e in range(N_EXPERTS):
        out = out + combine[:, e:e + 1] * swiglu(h, wg[e], wu[e], wd[e])
    return out


def project_a(h, w_in, pos):
    b, t = h.shape[:2]
    sizes = [QW_A, KV_A * HD_A, KV_A * HD_A, H_IDX * HD_IDX, HD_IDX, H_IDX]
    q, k, v, qi, ki, wi = jnp.split(h @ w_in, np.cumsum(sizes)[:-1].tolist(), axis=-1)
    q = rope(q.reshape(b, t, H_A, HD_A), pos, ROT_A).reshape(b, t, KV_A, G_A, HD_A)
    k = rope(k.reshape(b, t, KV_A, HD_A), pos, ROT_A)
    v = v.reshape(b, t, KV_A, HD_A)
    qi = rope(qi.reshape(b, t, H_IDX, HD_IDX), pos, ROT_IDX)
    ki = rope(ki.reshape(b, t, 1, HD_IDX), pos, ROT_IDX)[:, :, 0]
    wi = wi * IDX_W_SCALE
    return q, k, v, qi, ki, wi


def indexer_select(qi, wi, ki, qpos, kpos, n_sel):
    dots = jnp.einsum('bqhd,bsd->bqhs', qi, ki)
    score = jnp.einsum('bqhs,bqh->bqs', jax.nn.relu(dots).astype(jnp.float32), wi.astype(jnp.float32))
    allowed = kpos[None, :] <= qpos[:, None]
    score = jnp.where(allowed[None], score, -jnp.inf)
    vals, idx = lax.top_k(score, n_sel)
    return idx, jnp.isfinite(vals)


def gathered_attention(q, ksel, vsel, valid):
    s = jnp.einsum('bqhgd,bqnhd->bqhgn', q, ksel).astype(jnp.float32) * (q.shape[-1] ** -0.5)
    s = jnp.where(valid[:, :, None, None, :], s, -jnp.inf)
    p = jax.nn.softmax(s, axis=-1).astype(vsel.dtype)
    o = jnp.einsum('bqhgn,bqnhd->bqhgd', p, vsel)
    return o.reshape(o.shape[0], o.shape[1], -1)


def window_mask(qpos, kpos):
    d = qpos[:, :, None] - kpos[:, None, :]
    return (kpos[:, None, :] >= 0) & (d >= 0) & (d < WINDOW)


def sink_attention(q, k, v, sinks, mask):
    s = jnp.einsum('bnqhgd,bnshd->bnhgqs', q, k).astype(jnp.float32) * (q.shape[-1] ** -0.5)
    s = jnp.where(mask[None, :, None, None], s, -jnp.inf)
    sk = sinks.astype(jnp.float32)[None, None, :, :, None, None]
    m = jnp.maximum(jnp.max(s, axis=-1, keepdims=True), sk)
    p = jnp.exp(s - m)
    p = p / (jnp.sum(p, axis=-1, keepdims=True) + jnp.exp(sk - m))
    return jnp.einsum('bnhgqs,bnshd->bnqhgd', p.astype(v.dtype), v)


def window_attention(q, ctx, sinks):
    k_ctx, v_ctx, mask = ctx[0], ctx[1], ctx[2]
    b, t = q.shape[:2]
    n = k_ctx.shape[1]
    qb = q.reshape(b, n, t // n, KV_B, G_B, HD_B)
    o = sink_attention(qb, k_ctx, v_ctx, sinks, mask)
    return o.reshape(b, t, QW_B)


def setup_inputs(seed: int = 0) -> dict:
    key = jax.random.key(seed)
    ks = iter(jax.random.split(key, 64))

    def nrm(shape, scale=1.0):
        return jax.random.normal(next(ks), shape, jnp.float32) * scale

    d = D_MODEL
    n_pages = PAST_LEN // PAGE_SIZE
    n_used = DEC_BATCH * n_pages
    n_pool = n_used + max(1, n_used // 4)
    w_buf = min(WINDOW, PAST_LEN)
    perm = jax.random.permutation(next(ks), n_pool)
    page_table = perm[:n_used].reshape(DEC_BATCH, n_pages).astype(jnp.int32)
    mod_s = 0.25 * d ** -0.5
    return {
        'x_prompt': nrm((BATCH, SEQ, d)),
        'x_sample': nrm((DEC_BATCH, DEC_SEQ, d)),
        'cache_k_a': nrm((N_A_LAYERS, n_pool, PAGE_SIZE, KV_A, HD_A)),
        'cache_v_a': nrm((N_A_LAYERS, n_pool, PAGE_SIZE, KV_A, HD_A)),
        'cache_kidx_a': nrm((N_A_LAYERS, n_pool, PAGE_SIZE, HD_IDX)),
        'state_k_b': nrm((DEC_BATCH, w_buf, KV_B, HD_B)),
        'state_v_b': nrm((DEC_BATCH, w_buf, KV_B, HD_B)),
        'page_table': page_table,
        'c_prompt': nrm((BATCH, d)),
        'c_sample': nrm((DEC_BATCH, d)),
        'g_norm': 1.0 + nrm((DEPTH, 2, d), 0.02),
        'w_mod': nrm((DEPTH, d, 6 * d), mod_s),
        'b_mod': nrm((DEPTH, 6 * d), 0.02),
        'w_in_a': nrm((N_A_LAYERS, d, IN_A), d ** -0.5),
        'w_out_a': nrm((N_A_LAYERS, QW_A, d), QW_A ** -0.5),
        'g_kv': 1.0 + nrm((d,), 0.02),
        'w_kvmod': nrm((d, 2 * d), mod_s),
        'b_kvmod': nrm((2 * d,), 0.02),
        'w_kv_b': nrm((d, KVW_B), d ** -0.5),
        'w_q_b': nrm((N_B_LAYERS, d, QW_B), d ** -0.5),
        'sinks_b': nrm((N_B_LAYERS, H_B), 0.5),
        'w_out_b': nrm((N_B_LAYERS, QW_B, d), QW_B ** -0.5),
        'w_gate_d': nrm((N_DENSE, d, D_FF), d ** -0.5),
        'w_up_d': nrm((N_DENSE, d, D_FF), d ** -0.5),
        'w_down_d': nrm((N_DENSE, D_FF, d), D_FF ** -0.5),
        'w_router': nrm((N_MOE, d, N_EXPERTS), d ** -0.5),
        'b_router': nrm((N_MOE, N_EXPERTS), 0.01),
        'w_gate_e': nrm((N_MOE, N_EXPERTS, d, D_FF_EXPERT), d ** -0.5),
        'w_up_e': nrm((N_MOE, N_EXPERTS, d, D_FF_EXPERT), d ** -0.5),
        'w_down_e': nrm((N_MOE, N_EXPERTS, D_FF_EXPERT, d), D_FF_EXPERT ** -0.5),
        'g_final': 1.0 + nrm((d,), 0.02),
        'w_fmod': nrm((d, 2 * d), mod_s),
        'b_fmod': nrm((2 * d,), 0.02),
    }


def reference(x_prompt, x_sample, cache_k_a, cache_v_a, cache_kidx_a, state_k_b, state_v_b, page_table,
              c_prompt, c_sample, g_norm, w_mod, b_mod, w_in_a, w_out_a, g_kv, w_kvmod, b_kvmod, w_kv_b,
              w_q_b, sinks_b, w_out_b, w_gate_d, w_up_d, w_down_d, w_router, b_router, w_gate_e, w_up_e,
              w_down_e, g_final, w_fmod, b_fmod):
    n_pages = page_table.shape[1]
    past = n_pages * PAGE_SIZE

    def prompt_sparse_attn(li, q, k, v, qi, wi, ki):
        b, t = q.shape[:2]
        n_sel = min(TOPK_MAX, t // 4)
        nb = t // Q_BLOCK
        kpos = jnp.arange(t)

        def blk(args):
            qb, qib, wib, qpos = args
            idx, valid = indexer_select(qib, wib, ki, qpos, kpos, n_sel)
            ksel = jax.vmap(lambda kk, ii: kk[ii])(k, idx)
            vsel = jax.vmap(lambda vv, ii: vv[ii])(v, idx)
            return gathered_attention(qb, ksel, vsel, valid)

        def to_blocks(a):
            return jnp.moveaxis(a.reshape((b, nb, Q_BLOCK) + a.shape[2:]), 1, 0)

        out = lax.map(blk, (to_blocks(q), to_blocks(qi), to_blocks(wi), jnp.arange(t).reshape(nb, Q_BLOCK)))
        return jnp.moveaxis(out, 0, 1).reshape(b, t, QW_A)

    def sample_sparse_attn(li, q, k, v, qi, wi, ki):
        db, s_new = q.shape[:2]
        n_sel = min(TOPK_MAX, (past + s_new) // 4)
        ki_past = cache_kidx_a[li, page_table].reshape(db, past, HD_IDX)
        ki_all = jnp.concatenate([ki_past, ki], axis=1)
        qpos = past + jnp.arange(s_new)
        kpos = jnp.arange(past + s_new)
        idx, valid = indexer_select(qi, wi, ki_all, qpos, kpos, n_sel)
        is_past = (idx < past)[..., None, None]
        ip = jnp.minimum(idx, past - 1)
        phys = jax.vmap(lambda pt, i: pt[i])(page_table, ip // PAGE_SIZE)
        off = ip % PAGE_SIZE
        inew = jnp.clip(idx - past, 0, s_new - 1)
        ksel = jnp.where(is_past, cache_k_a[li, phys, off], jax.vmap(lambda kk, i: kk[i])(k, inew))
        vsel = jnp.where(is_past, cache_v_a[li, phys, off], jax.vmap(lambda vv, i: vv[i])(v, inew))
        return gathered_attention(q, ksel, vsel, valid)

    def shared_kv(x, c, pos):
        b, t = x.shape[:2]
        shift, scale = jnp.split(c @ w_kvmod + b_kvmod, 2, axis=-1)
        h = adaln(x, g_kv, shift, scale)
        kb, vb = jnp.split(h @ w_kv_b, 2, axis=-1)
        kb = rope(kb.reshape(b, t, KV_B, HD_B), pos, ROT_B)
        return kb, vb.reshape(b, t, KV_B, HD_B)

    def prompt_ctx(kb, vb):
        b, t = kb.shape[:2]
        nb = t // WINDOW

        def band(a):
            ab = a.reshape(b, nb, WINDOW, KV_B, HD_B)
            prev = jnp.concatenate([jnp.zeros_like(ab[:, :1]), ab[:, :-1]], axis=1)
            return jnp.concatenate([prev, ab], axis=2)

        qpos = jnp.arange(t).reshape(nb, WINDOW)
        kpos = jnp.concatenate([qpos - WINDOW, qpos], axis=-1)
        wp = min(WINDOW, t)
        return (band(kb), band(vb), window_mask(qpos, kpos), kb[:, t - wp:], vb[:, t - wp:])

    def sample_ctx(kb, vb):
        s_new = kb.shape[1]
        w_buf = state_k_b.shape[1]
        k_ctx = jnp.concatenate([state_k_b, kb], axis=1)
        v_ctx = jnp.concatenate([state_v_b, vb], axis=1)
        kpos = (past - w_buf + jnp.arange(w_buf + s_new))[None]
        qpos = (past + jnp.arange(s_new))[None]
        return (k_ctx[:, None], v_ctx[:, None], window_mask(qpos, kpos), k_ctx[:, -w_buf:], v_ctx[:, -w_buf:])

    def trunk(x, c, pos, sparse_attn, make_ctx):
        b, t, d = x.shape
        rows_a = []
        ctx = None
        for l in range(DEPTH):
            if l == N_A_LAYERS:
                kb, vb = shared_kv(x, c, pos)
                ctx = make_ctx(kb, vb)
            sh1, sc1, g1, sh2, sc2, g2 = jnp.split(c @ w_mod[l] + b_mod[l], 6, axis=-1)
            h = adaln(x, g_norm[l, 0], sh1, sc1)
            if l < N_A_LAYERS:
                q, k, v, qi, ki, wi = project_a(h, w_in_a[l], pos)
                o = sparse_attn(l, q, k, v, qi, wi, ki) @ w_out_a[l]
                rows_a.append((k, v, ki))
            else:
                lb = l - N_A_LAYERS
                q = rope((h @ w_q_b[lb]).reshape(b, t, H_B, HD_B), pos, ROT_B).reshape(b, t, KV_B, G_B, HD_B)
                o = window_attention(q, ctx, sinks_b[lb].reshape(KV_B, G_B)) @ w_out_b[lb]
            x = x + g1[:, None, :] * o
            h = adaln(x, g_norm[l, 1], sh2, sc2).reshape(b * t, d)
            if l % 2 == 0:
                f = swiglu(h, w_gate_d[l // 2], w_up_d[l // 2], w_down_d[l // 2])
            else:
                f = moe_ffn(h, w_router[l // 2], b_router[l // 2], w_gate_e[l // 2], w_up_e[l // 2], w_down_e[l // 2])
            x = x + g2[:, None, :] * f.reshape(b, t, d)
        fsh, fsc = jnp.split(c @ w_fmod + b_fmod, 2, axis=-1)
        return adaln(x, g_final, fsh, fsc), rows_a, ctx

    pos_p = jnp.arange(x_prompt.shape[1])
    pos_s = past + jnp.arange(x_sample.shape[1])
    y_prompt, rows_p, ctx_p = trunk(x_prompt, c_prompt, pos_p, prompt_sparse_attn, prompt_ctx)
    y_sample, rows_s, ctx_s = trunk(x_sample, c_sample, pos_s, sample_sparse_attn, sample_ctx)

    new_k_a_prompt = jnp.stack([r[0] for r in rows_p])
    new_v_a_prompt = jnp.stack([r[1] for r in rows_p])
    new_kidx_a_prompt = jnp.stack([r[2] for r in rows_p])
    new_k_a_sample = jnp.stack([r[0] for r in rows_s])
    new_v_a_sample = jnp.stack([r[1] for r in rows_s])
    new_kidx_a_sample = jnp.stack([r[2] for r in rows_s])
    new_k_b_prompt = ctx_p[3]
    new_v_b_prompt = ctx_p[4]
    new_k_b_sample = ctx_s[3]
    new_v_b_sample = ctx_s[4]
    return (y_prompt, y_sample, new_k_a_prompt, new_v_a_prompt, new_kidx_a_prompt,
            new_k_a_sample, new_v_a_sample, new_kidx_a_sample,
            new_k_b_prompt, new_v_b_prompt, new_k_b_sample, new_v_b_sample)
```

```python
import functools
import math

import jax
import jax.numpy as jnp
import numpy as np
from jax import lax
from jax.experimental import pallas as pl
from jax.experimental.pallas import tpu as pltpu

D_MODEL = 2048
DEPTH = 4
PAGE_SIZE = 128
HD_A = 128
H_A = D_MODEL // HD_A
KV_A = 4
G_A = H_A // KV_A
QW_A = H_A * HD_A
H_IDX = 16
HD_IDX = 64
TOPK_MAX = 256
IDX_W_SCALE = (H_IDX * HD_IDX) ** -0.5
Q_BLOCK = 128
HD_B = 64
H_B = D_MODEL // HD_B
KV_B = 4
G_B = H_B // KV_B
QW_B = H_B * HD_B
KVW_B = 2 * KV_B * HD_B
WINDOW = 128
ROPE_THETA = 500000.0
ROT_A = HD_A // 4
ROT_IDX = HD_IDX // 4
ROT_B = HD_B // 4
N_EXPERTS = 8
TOP_K_EXPERTS = 2
EPS = 1e-6
N_A_LAYERS = DEPTH // 2

LANES = 128
VMEM_LIMIT = 56 * 1024 * 1024


def _params(*sem):
    return pltpu.CompilerParams(dimension_semantics=sem, vmem_limit_bytes=VMEM_LIMIT)


def _rope_table(pos, head_dim, rot):
    half = rot // 2
    inv = jnp.exp(-math.log(ROPE_THETA) * jnp.arange(half, dtype=jnp.float32) / half)
    ang = pos.astype(jnp.float32)[:, None] * inv[None, :]
    cos, sin = jnp.cos(ang), jnp.sin(ang)
    d = np.arange(LANES) % head_dim
    idx = np.where(d < half, d, np.clip(d - half, 0, half - 1))
    cos_l, sin_l = cos[:, idx], sin[:, idx]
    c = jnp.where((d < rot)[None], cos_l, 1.0)
    u = jnp.where(((d >= half) & (d < rot))[None], sin_l, 0.0)
    dn = jnp.where((d < half)[None], -sin_l, 0.0)
    return jnp.concatenate([c, u, dn], axis=1)


def _tail_table(pos):
    t = _rope_table(pos, HD_IDX, ROT_IDX)
    lane = np.arange(LANES)
    c = jnp.where((lane < HD_IDX)[None], t[:, :LANES],
                  jnp.where((lane < HD_IDX + H_IDX)[None], IDX_W_SCALE, 1.0))
    keep = (lane < HD_IDX)[None]
    u = jnp.where(keep, t[:, LANES:2 * LANES], 0.0)
    dn = jnp.where(keep, t[:, 2 * LANES:], 0.0)
    return jnp.concatenate([c, u, dn], axis=1)


def _adaln_kernel(*refs, has_router):
    x_ref, g_ref, sh_ref, sc_ref = refs[:4]
    x = x_ref[...]
    y = x * lax.rsqrt(jnp.mean(x * x, axis=-1, keepdims=True) + EPS)
    h = (y * g_ref[...]) * (1.0 + sc_ref[0]) + sh_ref[0]
    if has_router:
        wr_ref, br_ref, h_ref, lg_ref = refs[4:]
        hb = h.astype(jnp.bfloat16)
        h_ref[...] = hb
        lg_ref[...] = jnp.dot(hb, wr_ref[...], preferred_element_type=jnp.float32) + br_ref[...]
    else:
        h_ref = refs[4]
        h_ref[...] = h.astype(h_ref.dtype)


def _adaln(x, g, shift, scale, *, tm, tiles_per_seq, out_dtype=jnp.bfloat16, router=None):
    m, d = x.shape
    r = shift.shape[1]
    in_specs = [
        pl.BlockSpec((tm, d), lambda i: (i, 0)),
        pl.BlockSpec((1, d), lambda i: (0, 0)),
        pl.BlockSpec((1, r, d), lambda i: (i // tiles_per_seq, 0, 0)),
        pl.BlockSpec((1, r, d), lambda i: (i // tiles_per_seq, 0, 0)),
    ]
    args = [x, g.reshape(1, d), shift, scale]
    out_shape = [jax.ShapeDtypeStruct((m, d), out_dtype)]
    out_specs = [pl.BlockSpec((tm, d), lambda i: (i, 0))]
    if router is not None:
        w_r, b_r = router
        wr = jnp.zeros((d, LANES), jnp.bfloat16).at[:, :N_EXPERTS].set(w_r.astype(jnp.bfloat16))
        br = jnp.zeros((1, LANES), jnp.float32).at[0, :N_EXPERTS].set(b_r.astype(jnp.float32))
        in_specs += [pl.BlockSpec((d, LANES), lambda i: (0, 0)), pl.BlockSpec((1, LANES), lambda i: (0, 0))]
        args += [wr, br]
        out_shape.append(jax.ShapeDtypeStruct((m, LANES), jnp.float32))
        out_specs.append(pl.BlockSpec((tm, LANES), lambda i: (i, 0)))
    outs = pl.pallas_call(
        functools.partial(_adaln_kernel, has_router=router is not None),
        grid=(m // tm,),
        in_specs=in_specs,
        out_specs=out_specs,
        out_shape=out_shape,
        compiler_params=_params("parallel"),
        name="adaln",
    )(*args)
    return outs if router is not None else outs[0]


def _mm_kernel(*refs, tn, has_bias, rope_half, has_res):
    it = iter(refs)
    a_ref = next(it)
    w_ref = next(it)
    bias_ref = next(it) if has_bias else None
    tab_ref = next(it) if rope_half else None
    res_ref = next(it) if has_res else None
    gate_ref = next(it) if has_res else None
    out_refs = list(it)
    acc = jnp.dot(a_ref[...], w_ref[...].astype(jnp.bfloat16), preferred_element_type=jnp.float32)
    if has_bias:
        acc = acc + bias_ref[...]
    if rope_half:
        c = tab_ref[:, :LANES]
        u = tab_ref[:, LANES:2 * LANES]
        dn = tab_ref[:, 2 * LANES:]
        parts = []
        for grp in range(tn // LANES):
            xg = acc[:, grp * LANES:(grp + 1) * LANES]
            parts.append(xg * c + pltpu.roll(xg, rope_half, 1) * u + pltpu.roll(xg, LANES - rope_half, 1) * dn)
        acc = parts[0] if len(parts) == 1 else jnp.concatenate(parts, axis=1)
    if has_res:
        acc = res_ref[...] + gate_ref[0] * acc
    for o in out_refs:
        o[...] = acc.astype(o.dtype)


def _matmul(a, w, *, layer=0, col0=0, ncols=None, tm, tn, bias=None, rope=None, res=None,
            out_dtypes=(jnp.float32,)):
    if w.ndim == 2:
        w = w[None]
    m, k = a.shape
    ncols = w.shape[2] - col0 if ncols is None else ncols
    assert ncols % tn == 0 and col0 % tn == 0 and m % tm == 0
    cb = col0 // tn
    in_specs = [
        pl.BlockSpec((tm, k), lambda i, j: (i, 0)),
        pl.BlockSpec((None, k, tn), lambda i, j: (layer, 0, cb + j)),
    ]
    args = [a, w]
    if bias is not None:
        in_specs.append(pl.BlockSpec((1, tn), lambda i, j: (0, cb + j)))
        args.append(bias)
    rope_half = 0
    if rope is not None:
        tab, rope_half = rope
        tab_tiles = tab.shape[0] // tm
        in_specs.append(pl.BlockSpec((tm, 3 * LANES), lambda i, j: (i % tab_tiles, 0)))
        args.append(tab)
    if res is not None:
        resid, gate, tps = res
        r = gate.shape[1]
        in_specs.append(pl.BlockSpec((tm, tn), lambda i, j: (i, j)))
        in_specs.append(pl.BlockSpec((1, r, tn), lambda i, j: (i // tps, 0, j)))
        args += [resid, gate]
    outs = pl.pallas_call(
        functools.partial(_mm_kernel, tn=tn, has_bias=bias is not None, rope_half=rope_half,
                          has_res=res is not None),
        grid=(m // tm, ncols // tn),
        in_specs=in_specs,
        out_specs=[pl.BlockSpec((tm, tn), lambda i, j: (i, j)) for _ in out_dtypes],
        out_shape=[jax.ShapeDtypeStruct((m, ncols), dt) for dt in out_dtypes],
        compiler_params=_params("parallel", "arbitrary"),
        name="matmul",
    )(*args)
    return outs if len(outs) > 1 else outs[0]


def _ffn_kernel(te_ref, tv_ref, *refs, nj, mode):
    del te_ref
    if mode == "dense":
        x_ref, wg_ref, wu_ref, wd_ref, res_ref, gate_ref, o_ref = refs
    else:
        x_ref, wg_ref, wu_ref, wd_ref, rs_ref, o_ref = refs
    i = pl.program_id(0)
    j = pl.program_id(1)

    @pl.when(j == 0)
    def _():
        o_ref[...] = jnp.zeros_like(o_ref)

    @pl.when(tv_ref[i] > 0)
    def _():
        x = x_ref[...]
        g = jnp.dot(x, wg_ref[...].astype(jnp.bfloat16), preferred_element_type=jnp.float32)
        u = jnp.dot(x, wu_ref[...].astype(jnp.bfloat16), preferred_element_type=jnp.float32)
        act = (g * (1.0 / (1.0 + jnp.exp(-g)))) * u
        o_ref[...] += jnp.dot(act.astype(jnp.bfloat16), wd_ref[...].astype(jnp.bfloat16),
                              preferred_element_type=jnp.float32)

    @pl.when(j == nj - 1)
    def _():
        if mode == "dense":
            o_ref[...] = res_ref[...] + gate_ref[0] * o_ref[...]
        else:
            o_ref[...] = rs_ref[...] * o_ref[...]


def _ffn(x, wg, wu, wd, *, layer, tm, tf, tile_expert=None, tile_valid=None, res=None, row_scale=None):
    m, d = x.shape
    f = wg.shape[-1]
    nt, nj = m // tm, f // tf
    dense = res is not None
    if dense:
        tile_expert = jnp.full((nt,), layer, jnp.int32)
        tile_valid = jnp.ones((nt,), jnp.int32)
    else:
        tile_expert = tile_expert + layer * wg.shape[1]
        wg, wu, wd = wg.reshape(-1, d, f), wu.reshape(-1, d, f), wd.reshape(-1, f, d)

    def jj(i, j, tv):
        return jnp.where(tv[i] > 0, j, nj - 1)

    in_specs = [
        pl.BlockSpec((tm, d), lambda i, j, te, tv: (i, 0)),
        pl.BlockSpec((None, d, tf), lambda i, j, te, tv: (te[i], 0, jj(i, j, tv))),
        pl.BlockSpec((None, d, tf), lambda i, j, te, tv: (te[i], 0, jj(i, j, tv))),
        pl.BlockSpec((None, tf, d), lambda i, j, te, tv: (te[i], jj(i, j, tv), 0)),
    ]
    args = [x, wg, wu, wd]
    if dense:
        resid, gate, tps = res
        r = gate.shape[1]
        in_specs.append(pl.BlockSpec((tm, d), lambda i, j, te, tv: (i, 0)))
        in_specs.append(pl.BlockSpec((1, r, d), lambda i, j, te, tv: (i // tps, 0, 0)))
        args += [resid, gate]
    else:
        in_specs.append(pl.BlockSpec((tm, 1), lambda i, j, te, tv: (i, 0)))
        args.append(row_scale)
    return pl.pallas_call(
        functools.partial(_ffn_kernel, nj=nj, mode="dense" if dense else "routed"),
        grid_spec=pltpu.PrefetchScalarGridSpec(
            num_scalar_prefetch=2,
            grid=(nt, nj),
            in_specs=in_specs,
            out_specs=pl.BlockSpec((tm, d), lambda i, j, te, tv: (i, 0)),
        ),
        out_shape=jax.ShapeDtypeStruct((m, d), jnp.float32),
        compiler_params=_params("parallel", "arbitrary"),
        name="ffn_dense" if dense else "ffn_routed",
    )(tile_expert, tile_valid, *args)


def _moe(h, logits, wg, wu, wd, *, layer, tm, tf):
    n, d = h.shape
    vals, idx = lax.top_k(logits[:, :N_EXPERTS], TOP_K_EXPERTS)
    gates = jax.nn.softmax(vals, axis=-1)
    npair = n * TOP_K_EXPERTS
    nt = npair // tm + N_EXPERTS
    flat_e = idx.reshape(-1).astype(jnp.int32)
    order = jnp.argsort(flat_e, stable=True)
    sorted_e = flat_e[order]
    counts = jnp.sum(flat_e[:, None] == jnp.arange(N_EXPERTS)[None, :], axis=0).astype(jnp.int32)
    tiles_e = (counts + tm - 1) // tm
    tile_end = jnp.cumsum(tiles_e)
    tile_start = tile_end - tiles_e
    grp_start = jnp.cumsum(counts) - counts
    rank = jnp.arange(npair, dtype=jnp.int32) - grp_start[sorted_e]
    dest_sorted = tile_start[sorted_e] * tm + rank
    dest = jnp.zeros((npair,), jnp.int32).at[order].set(dest_sorted)
    rows = nt * tm
    row_token = jnp.zeros((rows,), jnp.int32).at[dest].set(jnp.arange(npair, dtype=jnp.int32) // TOP_K_EXPERTS)
    row_gate = jnp.zeros((rows,), jnp.float32).at[dest].set(gates.reshape(-1))
    tile_ids = jnp.arange(nt, dtype=jnp.int32)
    total = tile_end[-1]
    tile_valid = (tile_ids < total).astype(jnp.int32)
    last_e = jnp.searchsorted(tile_end, total - 1, side="right").astype(jnp.int32)
    tile_expert = jnp.where(tile_valid > 0, jnp.searchsorted(tile_end, tile_ids, side="right").astype(jnp.int32),
                            last_e)
    tile_expert = jnp.clip(tile_expert, 0, N_EXPERTS - 1)
    xs = h[row_token]
    ys = _ffn(xs, wg, wu, wd, layer=layer, tm=tm, tf=tf, tile_expert=tile_expert, tile_valid=tile_valid,
              row_scale=row_gate[:, None])
    return ys[dest].reshape(n, TOP_K_EXPERTS, d).sum(axis=1)


def _indexer_select(qi, wi, ki, qpos, kpos, n_sel):
    dots = jnp.einsum('bqhd,bsd->bqhs', qi, ki)
    score = jnp.einsum('bqhs,bqh->bqs', jax.nn.relu(dots).astype(jnp.float32), wi.astype(jnp.float32))
    allowed = kpos[None, :] <= qpos[:, None]
    score = jnp.where(allowed[None], score, -jnp.inf)
    vals, idx = lax.top_k(score, n_sel)
    return idx, jnp.isfinite(vals)


def _gathered_attention(q, ksel, vsel, valid):
    s = jnp.einsum('bqhgd,bqnhd->bqhgn', q, ksel).astype(jnp.float32) * (q.shape[-1] ** -0.5)
    s = jnp.where(valid[:, :, None, None, :], s, -jnp.inf)
    p = jax.nn.softmax(s, axis=-1).astype(vsel.dtype)
    o = jnp.einsum('bqhgn,bqnhd->bqhgd', p, vsel)
    return o.reshape(o.shape[0], o.shape[1], -1)


def _window_mask(qpos, kpos):
    d = qpos[:, :, None] - kpos[:, None, :]
    return (kpos[:, None, :] >= 0) & (d >= 0) & (d < WINDOW)


def _sink_attention(q, k, v, sinks, mask):
    s = jnp.einsum('bnqhgd,bnshd->bnhgqs', q, k).astype(jnp.float32) * (q.shape[-1] ** -0.5)
    s = jnp.where(mask[None, :, None, None], s, -jnp.inf)
    sk = sinks.astype(jnp.float32)[None, None, :, :, None, None]
    m = jnp.maximum(jnp.max(s, axis=-1, keepdims=True), sk)
    p = jnp.exp(s - m)
    p = p / (jnp.sum(p, axis=-1, keepdims=True) + jnp.exp(sk - m))
    return jnp.einsum('bnhgqs,bnshd->bnqhgd', p.astype(v.dtype), v)


def _trunk(x, c_mod, kv_mod, f_mod, pos, weights, *, tm, sparse_attn, make_ctx):
    (g_norm, w_in_a, w_out_a, g_kv, w_kv_b, w_q_b, sinks_b, w_out_b, w_gate_d, w_up_d, w_down_d,
     w_router, b_router, w_gate_e, w_up_e, w_down_e, g_final) = weights
    b, t, d = x.shape
    m = b * t
    per_row = tm > t
    tps = 1 if per_row else t // tm

    def seq(v):
        if per_row:
            return jnp.repeat(v, t, axis=0)[None]
        return v[:, None, :]

    pos_rows = jnp.tile(pos, b) if per_row else pos
    tab_a = _rope_table(pos_rows, HD_A, ROT_A)
    tab_i = _rope_table(pos_rows, HD_IDX, ROT_IDX)
    tab_t = _tail_table(pos_rows)
    tab_b = _rope_table(pos_rows, HD_B, ROT_B)

    xf = x.reshape(m, d)
    rows_a = []
    ctx = None
    for l in range(DEPTH):
        if l == N_A_LAYERS:
            shift, scale = jnp.split(kv_mod, 2, axis=-1)
            h = _adaln(xf, g_kv, seq(shift), seq(scale), tm=tm, tiles_per_seq=tps)
            kb = _matmul(h, w_kv_b, col0=0, ncols=KV_B * HD_B, tm=tm, tn=KV_B * HD_B, rope=(tab_b, ROT_B // 2))
            vb = _matmul(h, w_kv_b, col0=KV_B * HD_B, ncols=KV_B * HD_B, tm=tm, tn=KV_B * HD_B)
            ctx = make_ctx(kb.reshape(b, t, KV_B, HD_B), vb.reshape(b, t, KV_B, HD_B))
        sh1, sc1, g1, sh2, sc2, g2 = jnp.split(c_mod[l], 6, axis=-1)
        h = _adaln(xf, g_norm[l, 0], seq(sh1), seq(sc1), tm=tm, tiles_per_seq=tps)
        if l < N_A_LAYERS:
            q = _matmul(h, w_in_a, layer=l, col0=0, ncols=QW_A, tm=tm, tn=512, rope=(tab_a, ROT_A // 2))
            k = _matmul(h, w_in_a, layer=l, col0=QW_A, ncols=512, tm=tm, tn=512, rope=(tab_a, ROT_A // 2))
            v = _matmul(h, w_in_a, layer=l, col0=QW_A + 512, ncols=512, tm=tm, tn=512)
            qi = _matmul(h, w_in_a, layer=l, col0=QW_A + 1024, ncols=1024, tm=tm, tn=512,
                         rope=(tab_i, ROT_IDX // 2))
            w_tail = jnp.pad(w_in_a[l][:, QW_A + 2048:], ((0, 0), (0, LANES - HD_IDX - H_IDX)))
            tail = _matmul(h, w_tail, tm=tm, tn=LANES, rope=(tab_t, ROT_IDX // 2))
            ki = tail[:, :HD_IDX]
            wi = tail[:, HD_IDX:HD_IDX + H_IDX]
            q5 = q.reshape(b, t, KV_A, G_A, HD_A)
            k4 = k.reshape(b, t, KV_A, HD_A)
            v4 = v.reshape(b, t, KV_A, HD_A)
            ki3 = ki.reshape(b, t, HD_IDX)
            o = sparse_attn(l, q5, k4, v4, qi.reshape(b, t, H_IDX, HD_IDX), wi.reshape(b, t, H_IDX), ki3)
            rows_a.append((k4, v4, ki3))
            w_o, lo = w_out_a, l
        else:
            lb = l - N_A_LAYERS
            q = _matmul(h, w_q_b, layer=lb, tm=tm, tn=512, rope=(tab_b, ROT_B // 2))
            k_ctx, v_ctx, mask = ctx[0], ctx[1], ctx[2]
            n = k_ctx.shape[1]
            qb = q.reshape(b, n, t // n, KV_B, G_B, HD_B)
            o = _sink_attention(qb, k_ctx, v_ctx, sinks_b[lb].reshape(KV_B, G_B), mask).reshape(b, t, QW_B)
            w_o, lo = w_out_b, lb
        xf = _matmul(o.reshape(m, -1).astype(jnp.bfloat16), w_o, layer=lo, tm=tm, tn=512,
                     res=(xf, seq(g1), tps))
        if l % 2 == 0:
            h = _adaln(xf, g_norm[l, 1], seq(sh2), seq(sc2), tm=tm, tiles_per_seq=tps)
            xf = _ffn(h, w_gate_d, w_up_d, w_down_d, layer=l // 2, tm=min(tm, 512), tf=256,
                      res=(xf, seq(g2), 1 if per_row else t // min(tm, 512)))
        else:
            h, logits = _adaln(xf, g_norm[l, 1], seq(sh2), seq(sc2), tm=tm, tiles_per_seq=tps,
                               router=(w_router[l // 2], b_router[l // 2]))
            f = _moe(h, logits, w_gate_e, w_up_e, w_down_e, layer=l // 2, tm=tm, tf=256)
            g2r = jnp.repeat(g2, t, axis=0)
            xf = xf + g2r * f
    fsh, fsc = jnp.split(f_mod, 2, axis=-1)
    y = _adaln(xf, g_final, seq(fsh), seq(fsc), tm=tm, tiles_per_seq=tps, out_dtype=jnp.float32)
    return y.reshape(b, t, d), rows_a, ctx


def kernel(x_prompt, x_sample, cache_k_a, cache_v_a, cache_kidx_a, state_k_b, state_v_b, page_table,
           c_prompt, c_sample, g_norm, w_mod, b_mod, w_in_a, w_out_a, g_kv, w_kvmod, b_kvmod, w_kv_b,
           w_q_b, sinks_b, w_out_b, w_gate_d, w_up_d, w_down_d, w_router, b_router, w_gate_e, w_up_e,
           w_down_e, g_final, w_fmod, b_fmod):
    n_pages = page_table.shape[1]
    past = n_pages * PAGE_SIZE
    nb_p, nb_s = c_prompt.shape[0], c_sample.shape[0]

    c_rows = 48
    c_all = jnp.zeros((c_rows, D_MODEL), jnp.bfloat16)
    c_all = c_all.at[:nb_p + nb_s].set(jnp.concatenate([c_prompt, c_sample], axis=0).astype(jnp.bfloat16))
    mods = [_matmul(c_all, w_mod, layer=l, tm=c_rows, tn=512, bias=b_mod[l][None]) for l in range(DEPTH)]
    kvm = _matmul(c_all, w_kvmod, tm=c_rows, tn=512, bias=b_kvmod[None])
    fm = _matmul(c_all, w_fmod, tm=c_rows, tn=512, bias=b_fmod[None])

    def prompt_sparse_attn(li, q, k, v, qi, wi, ki):
        b, t = q.shape[:2]
        n_sel = min(TOPK_MAX, t // 4)
        nb = t // Q_BLOCK
        kpos = jnp.arange(t)

        def blk(args):
            qb, qib, wib, qpos = args
            idx, valid = _indexer_select(qib, wib, ki, qpos, kpos, n_sel)
            ksel = jax.vmap(lambda kk, ii: kk[ii])(k, idx)
            vsel = jax.vmap(lambda vv, ii: vv[ii])(v, idx)
            return _gathered_attention(qb, ksel, vsel, valid)

        def to_blocks(a):
            return jnp.moveaxis(a.reshape((b, nb, Q_BLOCK) + a.shape[2:]), 1, 0)

        out = lax.map(blk, (to_blocks(q), to_blocks(qi), to_blocks(wi), jnp.arange(t).reshape(nb, Q_BLOCK)))
        return jnp.moveaxis(out, 0, 1).reshape(b, t, QW_A)

    def sample_sparse_attn(li, q, k, v, qi, wi, ki):
        db, s_new = q.shape[:2]
        n_sel = min(TOPK_MAX, (past + s_new) // 4)
        ki_past = cache_kidx_a[li, page_table].reshape(db, past, HD_IDX)
        ki_all = jnp.concatenate([ki_past, ki], axis=1)
        qpos = past + jnp.arange(s_new)
        kpos = jnp.arange(past + s_new)
        idx, valid = _indexer_select(qi, wi, ki_all, qpos, kpos, n_sel)
        is_past = (idx < past)[..., None, None]
        ip = jnp.minimum(idx, past - 1)
        phys = jax.vmap(lambda pt, i: pt[i])(page_table, ip // PAGE_SIZE)
        off = ip % PAGE_SIZE
        inew = jnp.clip(idx - past, 0, s_new - 1)
        ksel = jnp.where(is_past, cache_k_a[li, phys, off], jax.vmap(lambda kk, i: kk[i])(k, inew))
        vsel = jnp.where(is_past, cache_v_a[li, phys, off], jax.vmap(lambda vv, i: vv[i])(v, inew))
        return _gathered_attention(q, ksel, vsel, valid)

    def prompt_ctx(kb, vb):
        b, t = kb.shape[:2]
        nb = t // WINDOW

        def band(a):
            ab = a.reshape(b, nb, WINDOW, KV_B, HD_B)
            prev = jnp.concatenate([jnp.zeros_like(ab[:, :1]), ab[:, :-1]], axis=1)
            return jnp.concatenate([prev, ab], axis=2)

        qpos = jnp.arange(t).reshape(nb, WINDOW)
        kpos = jnp.concatenate([qpos - WINDOW, qpos], axis=-1)
        wp = min(WINDOW, t)
        return (band(kb), band(vb), _window_mask(qpos, kpos), kb[:, t - wp:], vb[:, t - wp:])

    def sample_ctx(kb, vb):
        s_new = kb.shape[1]
        w_buf = state_k_b.shape[1]
        k_ctx = jnp.concatenate([state_k_b, kb], axis=1)
        v_ctx = jnp.concatenate([state_v_b, vb], axis=1)
        kpos = (past - w_buf + jnp.arange(w_buf + s_new))[None]
        qpos = (past + jnp.arange(s_new))[None]
        return (k_ctx[:, None], v_ctx[:, None], _window_mask(qpos, kpos), k_ctx[:, -w_buf:], v_ctx[:, -w_buf:])

    weights = (g_norm, w_in_a, w_out_a, g_kv, w_kv_b, w_q_b, sinks_b, w_out_b, w_gate_d, w_up_d, w_down_d,
               w_router, b_router, w_gate_e, w_up_e, w_down_e, g_final)
    pos_p = jnp.arange(x_prompt.shape[1])
    pos_s = past + jnp.arange(x_sample.shape[1])
    y_prompt, rows_p, ctx_p = _trunk(
        x_prompt, [mm[:nb_p] for mm in mods], kvm[:nb_p], fm[:nb_p], pos_p, weights,
        tm=1024, sparse_attn=prompt_sparse_attn, make_ctx=prompt_ctx)
    y_sample, rows_s, ctx_s = _trunk(
        x_sample, [mm[nb_p:nb_p + nb_s] for mm in mods], kvm[nb_p:nb_p + nb_s], fm[nb_p:nb_p + nb_s], pos_s,
        weights, tm=x_sample.shape[0] * x_sample.shape[1], sparse_attn=sample_sparse_attn, make_ctx=sample_ctx)

    return (y_prompt, y_sample,
            jnp.stack([r[0] for r in rows_p]), jnp.stack([r[1] for r in rows_p]), jnp.stack([r[2] for r in rows_p]),
            jnp.stack([r[0] for r in rows_s]), jnp.stack([r[1] for r in rows_s]), jnp.stack([r[2] for r in rows_s]),
            ctx_p[3], ctx_p[4], ctx_s[3], ctx_s[4])
```

```python
import functools
import math

import jax
import jax.numpy as jnp
import numpy as np
from jax import lax
from jax.experimental import pallas as pl
from jax.experimental.pallas import tpu as pltpu

D_MODEL = 2048
DEPTH = 4
PAGE_SIZE = 128
HD_A = 128
H_A = D_MODEL // HD_A
KV_A = 4
G_A = H_A // KV_A
QW_A = H_A * HD_A
H_IDX = 16
HD_IDX = 64
TOPK_MAX = 256
IDX_W_SCALE = (H_IDX * HD_IDX) ** -0.5
Q_BLOCK = 128
HD_B = 64
H_B = D_MODEL // HD_B
KV_B = 4
G_B = H_B // KV_B
QW_B = H_B * HD_B
KVW_B = 2 * KV_B * HD_B
WINDOW = 128
ROPE_THETA = 500000.0
ROT_A = HD_A // 4
ROT_IDX = HD_IDX // 4
ROT_B = HD_B // 4
N_EXPERTS = 8
TOP_K_EXPERTS = 2
EPS = 1e-6
N_A_LAYERS = DEPTH // 2

LANES = 128
VMEM_LIMIT = 56 * 1024 * 1024


def _params(*sem):
    return pltpu.CompilerParams(dimension_semantics=sem, vmem_limit_bytes=VMEM_LIMIT)


def _rope_table(pos, head_dim, rot):
    half = rot // 2
    inv = jnp.exp(-math.log(ROPE_THETA) * jnp.arange(half, dtype=jnp.float32) / half)
    ang = pos.astype(jnp.float32)[:, None] * inv[None, :]
    cos, sin = jnp.cos(ang), jnp.sin(ang)
    d = np.arange(LANES) % head_dim
    idx = np.where(d < half, d, np.clip(d - half, 0, half - 1))
    cos_l, sin_l = cos[:, idx], sin[:, idx]
    c = jnp.where((d < rot)[None], cos_l, 1.0)
    u = jnp.where(((d >= half) & (d < rot))[None], sin_l, 0.0)
    dn = jnp.where((d < half)[None], -sin_l, 0.0)
    return jnp.concatenate([c, u, dn], axis=1)


def _tail_table(pos):
    t = _rope_table(pos, HD_IDX, ROT_IDX)
    lane = np.arange(LANES)
    c = jnp.where((lane < HD_IDX)[None], t[:, :LANES],
                  jnp.where((lane < HD_IDX + H_IDX)[None], IDX_W_SCALE, 1.0))
    keep = (lane < HD_IDX)[None]
    u = jnp.where(keep, t[:, LANES:2 * LANES], 0.0)
    dn = jnp.where(keep, t[:, 2 * LANES:], 0.0)
    return jnp.concatenate([c, u, dn], axis=1)


def _adaln_kernel(*refs, has_router):
    x_ref, g_ref, sh_ref, sc_ref = refs[:4]
    x = x_ref[...]
    y = x * lax.rsqrt(jnp.mean(x * x, axis=-1, keepdims=True) + EPS)
    h = (y * g_ref[...]) * (1.0 + sc_ref[0]) + sh_ref[0]
    if has_router:
        wr_ref, br_ref, h_ref, lg_ref = refs[4:]
        hb = h.astype(jnp.bfloat16)
        h_ref[...] = hb
        lg_ref[...] = jnp.dot(hb, wr_ref[...], preferred_element_type=jnp.float32) + br_ref[...]
    else:
        h_ref = refs[4]
        h_ref[...] = h.astype(h_ref.dtype)


def _adaln(x, g, shift, scale, *, tm, tiles_per_seq, out_dtype=jnp.bfloat16, router=None):
    m, d = x.shape
    r = shift.shape[1]
    in_specs = [
        pl.BlockSpec((tm, d), lambda i: (i, 0)),
        pl.BlockSpec((1, d), lambda i: (0, 0)),
        pl.BlockSpec((1, r, d), lambda i: (i // tiles_per_seq, 0, 0)),
        pl.BlockSpec((1, r, d), lambda i: (i // tiles_per_seq, 0, 0)),
    ]
    args = [x, g.reshape(1, d), shift, scale]
    out_shape = [jax.ShapeDtypeStruct((m, d), out_dtype)]
    out_specs = [pl.BlockSpec((tm, d), lambda i: (i, 0))]
    if router is not None:
        w_r, b_r = router
        wr = jnp.zeros((d, LANES), jnp.bfloat16).at[:, :N_EXPERTS].set(w_r.astype(jnp.bfloat16))
        br = jnp.zeros((1, LANES), jnp.float32).at[0, :N_EXPERTS].set(b_r.astype(jnp.float32))
        in_specs += [pl.BlockSpec((d, LANES), lambda i: (0, 0)), pl.BlockSpec((1, LANES), lambda i: (0, 0))]
        args += [wr, br]
        out_shape.append(jax.ShapeDtypeStruct((m, LANES), jnp.float32))
        out_specs.append(pl.BlockSpec((tm, LANES), lambda i: (i, 0)))
    outs = pl.pallas_call(
        functools.partial(_adaln_kernel, has_router=router is not None),
        grid=(m // tm,),
        in_specs=in_specs,
        out_specs=out_specs,
        out_shape=out_shape,
        compiler_params=_params("parallel"),
        name="adaln",
    )(*args)
    return outs if router is not None else outs[0]


def _mm_kernel(*refs, tn, has_bias, rope_half, has_res):
    it = iter(refs)
    a_ref = next(it)
    w_ref = next(it)
    bias_ref = next(it) if has_bias else None
    tab_ref = next(it) if rope_half else None
    res_ref = next(it) if has_res else None
    gate_ref = next(it) if has_res else None
    out_refs = list(it)
    acc = jnp.dot(a_ref[...], w_ref[...].astype(jnp.bfloat16), preferred_element_type=jnp.float32)
    if has_bias:
        acc = acc + bias_ref[...]
    if rope_half:
        c = tab_ref[:, :LANES]
        u = tab_ref[:, LANES:2 * LANES]
        dn = tab_ref[:, 2 * LANES:]
        parts = []
        for grp in range(tn // LANES):
            xg = acc[:, grp * LANES:(grp + 1) * LANES]
            parts.append(xg * c + pltpu.roll(xg, rope_half, 1) * u + pltpu.roll(xg, LANES - rope_half, 1) * dn)
        acc = parts[0] if len(parts) == 1 else jnp.concatenate(parts, axis=1)
    if has_res:
        acc = res_ref[...] + gate_ref[0] * acc
    for o in out_refs:
        o[...] = acc.astype(o.dtype)


def _matmul(a, w, *, layer=0, col0=0, ncols=None, tm, tn, bias=None, rope=None, res=None,
            out_dtypes=(jnp.float32,)):
    if w.ndim == 2:
        w = w[None]
    m, k = a.shape
    ncols = w.shape[2] - col0 if ncols is None else ncols
    assert ncols % tn == 0 and col0 % tn == 0 and m % tm == 0
    cb = col0 // tn
    in_specs = [
        pl.BlockSpec((tm, k), lambda i, j: (i, 0)),
        pl.BlockSpec((None, k, tn), lambda i, j: (layer, 0, cb + j)),
    ]
    args = [a, w]
    if bias is not None:
        in_specs.append(pl.BlockSpec((1, tn), lambda i, j: (0, cb + j)))
        args.append(bias)
    rope_half = 0
    if rope is not None:
        tab, rope_half = rope
        tab_tiles = tab.shape[0] // tm
        in_specs.append(pl.BlockSpec((tm, 3 * LANES), lambda i, j: (i % tab_tiles, 0)))
        args.append(tab)
    if res is not None:
        resid, gate, tps = res
        r = gate.shape[1]
        in_specs.append(pl.BlockSpec((tm, tn), lambda i, j: (i, j)))
        in_specs.append(pl.BlockSpec((1, r, tn), lambda i, j: (i // tps, 0, j)))
        args += [resid, gate]
    outs = pl.pallas_call(
        functools.partial(_mm_kernel, tn=tn, has_bias=bias is not None, rope_half=rope_half,
                          has_res=res is not None),
        grid=(m // tm, ncols // tn),
        in_specs=in_specs,
        out_specs=[pl.BlockSpec((tm, tn), lambda i, j: (i, j)) for _ in out_dtypes],
        out_shape=[jax.ShapeDtypeStruct((m, ncols), dt) for dt in out_dtypes],
        compiler_params=_params("parallel", "arbitrary"),
        name="matmul",
    )(*args)
    return outs if len(outs) > 1 else outs[0]


def _ffn_kernel(te_ref, tv_ref, *refs, nj, mode):
    del te_ref
    if mode == "dense":
        x_ref, wg_ref, wu_ref, wd_ref, res_ref, gate_ref, o_ref = refs
    else:
        x_ref, wg_ref, wu_ref, wd_ref, rs_ref, o_ref = refs
    i = pl.program_id(0)
    j = pl.program_id(1)

    @pl.when(j == 0)
    def _():
        o_ref[...] = jnp.zeros_like(o_ref)

    @pl.when(tv_ref[i] > 0)
    def _():
        x = x_ref[...]
        g = jnp.dot(x, wg_ref[...].astype(jnp.bfloat16), preferred_element_type=jnp.float32)
        u = jnp.dot(x, wu_ref[...].astype(jnp.bfloat16), preferred_element_type=jnp.float32)
        act = (g * (1.0 / (1.0 + jnp.exp(-g)))) * u
        o_ref[...] += jnp.dot(act.astype(jnp.bfloat16), wd_ref[...].astype(jnp.bfloat16),
                              preferred_element_type=jnp.float32)

    @pl.when(j == nj - 1)
    def _():
        if mode == "dense":
            o_ref[...] = res_ref[...] + gate_ref[0] * o_ref[...]
        else:
            o_ref[...] = rs_ref[...] * o_ref[...]


def _ffn(x, wg, wu, wd, *, layer, tm, tf, tile_expert=None, tile_valid=None, res=None, row_scale=None):
    m, d = x.shape
    f = wg.shape[-1]
    nt, nj = m // tm, f // tf
    dense = res is not None
    if dense:
        tile_expert = jnp.full((nt,), layer, jnp.int32)
        tile_valid = jnp.ones((nt,), jnp.int32)
    else:
        tile_expert = tile_expert + layer * wg.shape[1]
        wg, wu, wd = wg.reshape(-1, d, f), wu.reshape(-1, d, f), wd.reshape(-1, f, d)

    def jj(i, j, tv):
        return jnp.where(tv[i] > 0, j, nj - 1)

    in_specs = [
        pl.BlockSpec((tm, d), lambda i, j, te, tv: (i, 0)),
        pl.BlockSpec((None, d, tf), lambda i, j, te, tv: (te[i], 0, jj(i, j, tv))),
        pl.BlockSpec((None, d, tf), lambda i, j, te, tv: (te[i], 0, jj(i, j, tv))),
        pl.BlockSpec((None, tf, d), lambda i, j, te, tv: (te[i], jj(i, j, tv), 0)),
    ]
    args = [x, wg, wu, wd]
    if dense:
        resid, gate, tps = res
        r = gate.shape[1]
        in_specs.append(pl.BlockSpec((tm, d), lambda i, j, te, tv: (i, 0)))
        in_specs.append(pl.BlockSpec((1, r, d), lambda i, j, te, tv: (i // tps, 0, 0)))
        args += [resid, gate]
    else:
        in_specs.append(pl.BlockSpec((tm, 1), lambda i, j, te, tv: (i, 0)))
        args.append(row_scale)
    return pl.pallas_call(
        functools.partial(_ffn_kernel, nj=nj, mode="dense" if dense else "routed"),
        grid_spec=pltpu.PrefetchScalarGridSpec(
            num_scalar_prefetch=2,
            grid=(nt, nj),
            in_specs=in_specs,
            out_specs=pl.BlockSpec((tm, d), lambda i, j, te, tv: (i, 0)),
        ),
        out_shape=jax.ShapeDtypeStruct((m, d), jnp.float32),
        compiler_params=_params("parallel", "arbitrary"),
        name="ffn_dense" if dense else "ffn_routed",
    )(tile_expert, tile_valid, *args)


def _moe(h, logits, wg, wu, wd, *, layer, tm, tf):
    n, d = h.shape
    vals, idx = lax.top_k(logits[:, :N_EXPERTS], TOP_K_EXPERTS)
    gates = jax.nn.softmax(vals, axis=-1)
    npair = n * TOP_K_EXPERTS
    nt = npair // tm + N_EXPERTS
    flat_e = idx.reshape(-1).astype(jnp.int32)
    order = jnp.argsort(flat_e, stable=True)
    sorted_e = flat_e[order]
    counts = jnp.sum(flat_e[:, None] == jnp.arange(N_EXPERTS)[None, :], axis=0).astype(jnp.int32)
    tiles_e = (counts + tm - 1) // tm
    tile_end = jnp.cumsum(tiles_e)
    tile_start = tile_end - tiles_e
    grp_start = jnp.cumsum(counts) - counts
    rank = jnp.arange(npair, dtype=jnp.int32) - grp_start[sorted_e]
    dest_sorted = tile_start[sorted_e] * tm + rank
    dest = jnp.zeros((npair,), jnp.int32).at[order].set(dest_sorted)
    rows = nt * tm
    row_token = jnp.zeros((rows,), jnp.int32).at[dest].set(jnp.arange(npair, dtype=jnp.int32) // TOP_K_EXPERTS)
    row_gate = jnp.zeros((rows,), jnp.float32).at[dest].set(gates.reshape(-1))
    tile_ids = jnp.arange(nt, dtype=jnp.int32)
    total = tile_end[-1]
    tile_valid = (tile_ids < total).astype(jnp.int32)
    last_e = jnp.searchsorted(tile_end, total - 1, side="right").astype(jnp.int32)
    tile_expert = jnp.where(tile_valid > 0, jnp.searchsorted(tile_end, tile_ids, side="right").astype(jnp.int32),
                            last_e)
    tile_expert = jnp.clip(tile_expert, 0, N_EXPERTS - 1)
    xs = h[row_token]
    ys = _ffn(xs, wg, wu, wd, layer=layer, tm=tm, tf=tf, tile_expert=tile_expert, tile_valid=tile_valid,
              row_scale=row_gate[:, None])
    return ys[dest].reshape(n, TOP_K_EXPERTS, d).sum(axis=1)


INT_MIN = -(2 ** 31)
NEG = -1e30


def _sortable(score):
    bits = pltpu.bitcast(score, jnp.int32)
    return bits ^ ((bits >> 31) & jnp.int32(0x7FFFFFFF))


def _kth_largest(count_ge, rows, n_sel):
    def step(s, t):
        cand = t + lax.shift_left(jnp.int32(1), 31 - s)
        return jnp.where(count_ge(cand) >= n_sel, cand, t)

    return lax.fori_loop(0, 32, step, jnp.full((rows, 1), INT_MIN, jnp.int32))


def _psa_kernel(q_ref, qi_ref, k_ref, v_ref, tail_ref, o_ref, key_scr, m_scr, l_scr, acc_scr, *, tq, tk, n_sel):
    i = pl.program_id(1)
    nck = (i + 1) * (tq // tk)
    row0 = pl.multiple_of(i * tq, tq)
    nt = (((1,), (1,)), ((), ()))
    lane = lax.broadcasted_iota(jnp.int32, (tk, LANES), 1)
    rows = row0 + lax.broadcasted_iota(jnp.int32, (tq, tk), 0)
    cols0 = lax.broadcasted_iota(jnp.int32, (tq, tk), 1)
    w_all = tail_ref[pl.ds(row0, tq), :]

    def score_chunk(c, carry):
        ks = pl.multiple_of(c * tk, tk)
        kf = jnp.where(lane < HD_IDX, tail_ref[pl.ds(ks, tk), :], 0.0)
        k_even = kf.astype(jnp.bfloat16)
        k_odd = pltpu.roll(kf, HD_IDX, 1).astype(jnp.bfloat16)
        sc = jnp.zeros((tq, tk), jnp.float32)
        for h in range(H_IDX):
            qp = qi_ref[:, (h // 2) * LANES:(h // 2 + 1) * LANES]
            d = lax.dot_general(qp, k_even if h % 2 == 0 else k_odd, nt, preferred_element_type=jnp.float32)
            sc = sc + jnp.maximum(d, 0.0) * w_all[:, HD_IDX + h:HD_IDX + h + 1]
        key_scr[c] = jnp.where(ks + cols0 <= rows, _sortable(sc), INT_MIN)
        return carry

    lax.fori_loop(0, nck, score_chunk, 0)

    def count_ge(cand):
        def chunk(c, acc):
            hit = jnp.where(key_scr[c] >= cand, 1.0, 0.0)
            for s in range(tk // LANES):
                acc = acc + hit[:, s * LANES:(s + 1) * LANES]
            return acc

        acc = lax.fori_loop(0, nck, chunk, jnp.zeros((tq, LANES), jnp.float32))
        return jnp.sum(acc, axis=1, keepdims=True)

    thr = jnp.maximum(_kth_largest(count_ge, tq, float(n_sel)), INT_MIN + 1)

    m_scr[...] = jnp.full_like(m_scr, NEG)
    l_scr[...] = jnp.zeros_like(l_scr)
    acc_scr[...] = jnp.zeros_like(acc_scr)
    scale = HD_A ** -0.5

    def att_chunk(c, carry):
        ks = pl.multiple_of(c * tk, tk)
        bias = jnp.where(key_scr[c] >= thr, 0.0, NEG)
        bias = jnp.concatenate([bias] * G_A, axis=0)
        for g in range(KV_A):
            qg = jnp.concatenate(
                [q_ref[:, (g * G_A + j) * HD_A:(g * G_A + j + 1) * HD_A] for j in range(G_A)], axis=0)
            kg = k_ref[pl.ds(ks, tk), g * HD_A:(g + 1) * HD_A]
            s = lax.dot_general(qg, kg, nt, preferred_element_type=jnp.float32) * scale + bias
            m_old = m_scr[g]
            m_new = jnp.maximum(m_old, jnp.max(s, axis=1, keepdims=True))
            alpha = jnp.exp(m_old - m_new)
            p = jnp.exp(s - m_new)
            l_scr[g] = alpha * l_scr[g] + jnp.sum(p, axis=1, keepdims=True)
            vg = v_ref[pl.ds(ks, tk), g * HD_A:(g + 1) * HD_A]
            acc_scr[g] = alpha * acc_scr[g] + jnp.dot(p.astype(jnp.bfloat16), vg,
                                                      preferred_element_type=jnp.float32)
            m_scr[g] = m_new
        return carry

    lax.fori_loop(0, nck, att_chunk, 0)

    for g in range(KV_A):
        og = acc_scr[g] / l_scr[g]
        for j in range(G_A):
            h = g * G_A + j
            o_ref[:, h * HD_A:(h + 1) * HD_A] = og[j * tq:(j + 1) * tq].astype(o_ref.dtype)


def _prompt_sparse_attn(q, qi, k, v, tail, *, b, t, tq, tk, n_sel):
    nqt = t // tq
    return pl.pallas_call(
        functools.partial(_psa_kernel, tq=tq, tk=tk, n_sel=n_sel),
        grid=(b, nqt),
        in_specs=[
            pl.BlockSpec((tq, q.shape[1]), lambda bb, i: (bb * nqt + i, 0)),
            pl.BlockSpec((tq, qi.shape[1]), lambda bb, i: (bb * nqt + i, 0)),
            pl.BlockSpec((t, k.shape[1]), lambda bb, i: (bb, 0)),
            pl.BlockSpec((t, v.shape[1]), lambda bb, i: (bb, 0)),
            pl.BlockSpec((t, LANES), lambda bb, i: (bb, 0)),
        ],
        out_specs=pl.BlockSpec((tq, q.shape[1]), lambda bb, i: (bb * nqt + i, 0)),
        out_shape=jax.ShapeDtypeStruct(q.shape, jnp.bfloat16),
        scratch_shapes=[
            pltpu.VMEM((t // tk, tq, tk), jnp.int32),
            pltpu.VMEM((KV_A, G_A * tq, 1), jnp.float32),
            pltpu.VMEM((KV_A, G_A * tq, 1), jnp.float32),
            pltpu.VMEM((KV_A, G_A * tq, HD_A), jnp.float32),
        ],
        compiler_params=_params("parallel", "arbitrary"),
        name="prompt_sparse_attn",
    )(q, qi, k, v, tail)


QPAD = 8
PAGES_PER_STEP = 8


def _ssi_kernel(pt_ref, qi_ref, w_ref, kn_ref, *refs, pp, n_sel, s_new, ns):
    del pt_ref
    page_refs = refs[:pp]
    bias_ref, bias_new_ref, key_scr = refs[pp:]
    s = pl.program_id(1)
    n = pp * PAGE_SIZE
    nt = (((1,), (1,)), ((), ()))

    def scores(kk):
        d = lax.dot_general(qi_ref[...], kk, nt, preferred_element_type=jnp.float32)
        sc = jnp.maximum(d, 0.0) * w_ref[...]
        return jnp.sum(sc.reshape(QPAD, H_IDX, kk.shape[0]), axis=1)

    kk = jnp.concatenate([r[...].astype(jnp.bfloat16) for r in page_refs], axis=0)
    qrow = lax.broadcasted_iota(jnp.int32, (QPAD, n), 0)
    key_scr[s] = jnp.where(qrow < s_new, _sortable(scores(kk)), INT_MIN)

    @pl.when(s == ns - 1)
    def _():
        qr = lax.broadcasted_iota(jnp.int32, (QPAD, PAGE_SIZE), 0)
        col = lax.broadcasted_iota(jnp.int32, (QPAD, PAGE_SIZE), 1)
        key_new = jnp.where((col <= qr) & (qr < s_new), _sortable(scores(kn_ref[...])), INT_MIN)

        def count_ge(cand):
            acc = jnp.where(key_new >= cand, 1.0, 0.0)
            for c in range(ns):
                hit = jnp.where(key_scr[c] >= cand, 1.0, 0.0)
                for u in range(n // LANES):
                    acc = acc + hit[:, u * LANES:(u + 1) * LANES]
            return jnp.sum(acc, axis=1, keepdims=True)

        thr = jnp.maximum(_kth_largest(count_ge, QPAD, float(n_sel)), INT_MIN + 1)
        for c in range(ns):
            bias_ref[c] = jnp.where(key_scr[c] >= thr, 0.0, NEG)
        bias_new_ref[...] = jnp.where(key_new >= thr, 0.0, NEG)


def _ssa_kernel(pt_ref, q_ref, bias_ref, bias_new_ref, kn_ref, vn_ref, *refs, pp, ns):
    del pt_ref
    k_refs = refs[:pp]
    v_refs = refs[pp:2 * pp]
    o_ref, m_scr, l_scr, acc_scr = refs[2 * pp:]
    s = pl.program_id(1)
    nt = (((1,), (1,)), ((), ()))
    scale = HD_A ** -0.5

    @pl.when(s == 0)
    def _():
        m_scr[...] = jnp.full_like(m_scr, NEG)
        l_scr[...] = jnp.zeros_like(l_scr)
        acc_scr[...] = jnp.zeros_like(acc_scr)

    def update(kk, vv, bias):
        bias = jnp.concatenate([bias] * G_A, axis=0)
        for g in range(KV_A):
            kg = kk[:, g * HD_A:(g + 1) * HD_A]
            sc = lax.dot_general(q_ref[g], kg, nt, preferred_element_type=jnp.float32) * scale + bias
            m_old = m_scr[g]
            m_new = jnp.maximum(m_old, jnp.max(sc, axis=1, keepdims=True))
            alpha = jnp.exp(m_old - m_new)
            p = jnp.exp(sc - m_new)
            l_scr[g] = alpha * l_scr[g] + jnp.sum(p, axis=1, keepdims=True)
            acc_scr[g] = alpha * acc_scr[g] + jnp.dot(p.astype(jnp.bfloat16), vv[:, g * HD_A:(g + 1) * HD_A],
                                                      preferred_element_type=jnp.float32)
            m_scr[g] = m_new

    kk = jnp.concatenate([r[...].astype(jnp.bfloat16) for r in k_refs], axis=0)
    vv = jnp.concatenate([r[...].astype(jnp.bfloat16) for r in v_refs], axis=0)
    update(kk, vv, bias_ref[...])

    @pl.when(s == ns - 1)
    def _():
        update(kn_ref[...], vn_ref[...], bias_new_ref[...])
        for g in range(KV_A):
            o_ref[g] = acc_scr[g] / l_scr[g]


def _sample_sparse_attn(q, qi, wi, ki, k, v, cache_k, cache_v, cache_kidx, page_table, *, layer, n_sel):
    db, s_new = q.shape[:2]
    n_pages = page_table.shape[1]
    pp = PAGES_PER_STEP
    ns = n_pages // pp
    n = pp * PAGE_SIZE
    kvw = KV_A * HD_A
    pad_q = ((0, 0), (0, QPAD - s_new), (0, 0), (0, 0))
    pad_k = ((0, 0), (0, PAGE_SIZE - s_new), (0, 0))
    qi_r = jnp.pad(qi.reshape(db, s_new, H_IDX, HD_IDX), pad_q).reshape(db, QPAD * H_IDX, HD_IDX)
    w_r = jnp.pad(wi, ((0, 0), (0, QPAD - s_new), (0, 0))).reshape(db, QPAD * H_IDX, 1)
    ki_n = jnp.pad(ki.astype(jnp.bfloat16), pad_k)
    k_n = jnp.pad(k, pad_k)
    v_n = jnp.pad(v, pad_k)
    q_r = jnp.pad(q.reshape(db, s_new, KV_A, G_A, HD_A).transpose(0, 2, 3, 1, 4),
                  ((0, 0), (0, 0), (0, 0), (0, QPAD - s_new), (0, 0))).reshape(db, KV_A, G_A * QPAD, HD_A)

    def page_spec(width, j):
        return pl.BlockSpec((None, None, PAGE_SIZE, width),
                            lambda b, s, pt: (layer, pt[b, s * pp + j], 0, 0))

    bias, bias_new = pl.pallas_call(
        functools.partial(_ssi_kernel, pp=pp, n_sel=n_sel, s_new=s_new, ns=ns),
        grid_spec=pltpu.PrefetchScalarGridSpec(
            num_scalar_prefetch=1,
            grid=(db, ns),
            in_specs=[
                pl.BlockSpec((None, QPAD * H_IDX, HD_IDX), lambda b, s, pt: (b, 0, 0)),
                pl.BlockSpec((None, QPAD * H_IDX, 1), lambda b, s, pt: (b, 0, 0)),
                pl.BlockSpec((None, PAGE_SIZE, HD_IDX), lambda b, s, pt: (b, 0, 0)),
            ] + [page_spec(HD_IDX, j) for j in range(pp)],
            out_specs=[
                pl.BlockSpec((None, ns, QPAD, n), lambda b, s, pt: (b, 0, 0, 0)),
                pl.BlockSpec((None, QPAD, PAGE_SIZE), lambda b, s, pt: (b, 0, 0)),
            ],
            scratch_shapes=[pltpu.VMEM((ns, QPAD, n), jnp.int32)],
        ),
        out_shape=[jax.ShapeDtypeStruct((db, ns, QPAD, n), jnp.float32),
                   jax.ShapeDtypeStruct((db, QPAD, PAGE_SIZE), jnp.float32)],
        compiler_params=_params("parallel", "arbitrary"),
        name="sample_indexer",
    )(page_table, qi_r, w_r, ki_n, *([cache_kidx] * pp))

    ck = cache_k.reshape(cache_k.shape[:3] + (kvw,))
    cv = cache_v.reshape(cache_v.shape[:3] + (kvw,))
    o = pl.pallas_call(
        functools.partial(_ssa_kernel, pp=pp, ns=ns),
        grid_spec=pltpu.PrefetchScalarGridSpec(
            num_scalar_prefetch=1,
            grid=(db, ns),
            in_specs=[
                pl.BlockSpec((None, KV_A, G_A * QPAD, HD_A), lambda b, s, pt: (b, 0, 0, 0)),
                pl.BlockSpec((None, None, QPAD, n), lambda b, s, pt: (b, s, 0, 0)),
                pl.BlockSpec((None, QPAD, PAGE_SIZE), lambda b, s, pt: (b, 0, 0)),
                pl.BlockSpec((None, PAGE_SIZE, kvw), lambda b, s, pt: (b, 0, 0)),
                pl.BlockSpec((None, PAGE_SIZE, kvw), lambda b, s, pt: (b, 0, 0)),
            ] + [page_spec(kvw, j) for j in range(pp)] * 2,
            out_specs=pl.BlockSpec((None, KV_A, G_A * QPAD, HD_A), lambda b, s, pt: (b, 0, 0, 0)),
            scratch_shapes=[
                pltpu.VMEM((KV_A, G_A * QPAD, 1), jnp.float32),
                pltpu.VMEM((KV_A, G_A * QPAD, 1), jnp.float32),
                pltpu.VMEM((KV_A, G_A * QPAD, HD_A), jnp.float32),
            ],
        ),
        out_shape=jax.ShapeDtypeStruct((db, KV_A, G_A * QPAD, HD_A), jnp.float32),
        compiler_params=_params("parallel", "arbitrary"),
        name="sample_sparse_attn",
    )(page_table, q_r, bias, bias_new, k_n, v_n, *([ck] * pp), *([cv] * pp))
    o = o.reshape(db, KV_A, G_A, QPAD, HD_A)[:, :, :, :s_new].transpose(0, 3, 1, 2, 4)
    return o.reshape(db * s_new, H_A * HD_A).astype(jnp.bfloat16)


def _wa_kernel(q_ref, kp_ref, kc_ref, vp_ref, vc_ref, sink_ref, o_ref, *, tq, blocks_per_seq, mask_first):
    n = pl.program_id(0)
    nt = (((1,), (1,)), ((), ()))
    r = lax.broadcasted_iota(jnp.int32, (tq, 2 * WINDOW), 0)
    c = lax.broadcasted_iota(jnp.int32, (tq, 2 * WINDOW), 1)
    ok = (c - r >= 1) & (c - r <= WINDOW)
    if mask_first:
        ok = ok & ((n % blocks_per_seq != 0) | (c >= WINDOW))
    bias = jnp.where(ok, 0.0, NEG)
    bias = jnp.concatenate([bias] * G_B, axis=0)
    scale = HD_B ** -0.5
    for g in range(KV_B):
        cols = slice(g * HD_B, (g + 1) * HD_B)
        kcat = jnp.concatenate([kp_ref[:, cols], kc_ref[:, cols]], axis=0)
        vcat = jnp.concatenate([vp_ref[:, cols], vc_ref[:, cols]], axis=0)
        qg = jnp.concatenate(
            [q_ref[:, (g * G_B + j) * HD_B:(g * G_B + j + 1) * HD_B] for j in range(G_B)], axis=0)
        s = lax.dot_general(qg, kcat, nt, preferred_element_type=jnp.float32) * scale + bias
        sink = sink_ref[g]
        m = jnp.maximum(jnp.max(s, axis=1, keepdims=True), sink)
        p = jnp.exp(s - m)
        p = p / (jnp.sum(p, axis=1, keepdims=True) + jnp.exp(sink - m))
        og = jnp.dot(p.astype(jnp.bfloat16), vcat, preferred_element_type=jnp.float32)
        for j in range(G_B):
            h = g * G_B + j
            o_ref[:, h * HD_B:(h + 1) * HD_B] = og[j * tq:(j + 1) * tq].astype(o_ref.dtype)


def _window_attn(q, k_prev, k_cur, v_prev, v_cur, sinks, *, tq, blocks_per_seq, mask_first):
    nblk = q.shape[0] // tq
    kvw = KV_B * HD_B
    if mask_first:
        prev = lambda n: (jnp.maximum(n - 1, 0), 0)
    else:
        prev = lambda n: (n, 0)
    cur = lambda n: (n, 0)
    sink_rows = jnp.repeat(sinks.astype(jnp.float32).reshape(KV_B, G_B), tq, axis=1)[:, :, None]
    return pl.pallas_call(
        functools.partial(_wa_kernel, tq=tq, blocks_per_seq=blocks_per_seq, mask_first=mask_first),
        grid=(nblk,),
        in_specs=[
            pl.BlockSpec((tq, q.shape[1]), cur),
            pl.BlockSpec((WINDOW, kvw), prev),
            pl.BlockSpec((WINDOW, kvw), cur),
            pl.BlockSpec((WINDOW, kvw), prev),
            pl.BlockSpec((WINDOW, kvw), cur),
            pl.BlockSpec((KV_B, G_B * tq, 1), lambda n: (0, 0, 0)),
        ],
        out_specs=pl.BlockSpec((tq, q.shape[1]), cur),
        out_shape=jax.ShapeDtypeStruct(q.shape, jnp.bfloat16),
        compiler_params=_params("parallel"),
        name="window_attn",
    )(q, k_prev, k_cur, v_prev, v_cur, sink_rows)


def _trunk(x, c_mod, kv_mod, f_mod, pos, weights, *, tm, sparse_attn, make_ctx):
    (g_norm, w_in_a, w_out_a, g_kv, w_kv_b, w_q_b, sinks_b, w_out_b, w_gate_d, w_up_d, w_down_d,
     w_router, b_router, w_gate_e, w_up_e, w_down_e, g_final) = weights
    b, t, d = x.shape
    m = b * t
    per_row = tm > t
    tps = 1 if per_row else t // tm

    def seq(v):
        if per_row:
            return jnp.repeat(v, t, axis=0)[None]
        return v[:, None, :]

    pos_rows = jnp.tile(pos, b) if per_row else pos
    tab_a = _rope_table(pos_rows, HD_A, ROT_A)
    tab_i = _rope_table(pos_rows, HD_IDX, ROT_IDX)
    tab_t = _tail_table(pos_rows)
    tab_b = _rope_table(pos_rows, HD_B, ROT_B)

    xf = x.reshape(m, d)
    rows_a = []
    ctx = None
    for l in range(DEPTH):
        if l == N_A_LAYERS:
            shift, scale = jnp.split(kv_mod, 2, axis=-1)
            h = _adaln(xf, g_kv, seq(shift), seq(scale), tm=tm, tiles_per_seq=tps)
            kvw_b = KV_B * HD_B
            kb, kb_bf = _matmul(h, w_kv_b, col0=0, ncols=kvw_b, tm=tm, tn=kvw_b, rope=(tab_b, ROT_B // 2),
                                out_dtypes=(jnp.float32, jnp.bfloat16))
            vb, vb_bf = _matmul(h, w_kv_b, col0=kvw_b, ncols=kvw_b, tm=tm, tn=kvw_b,
                                out_dtypes=(jnp.float32, jnp.bfloat16))
            ctx = make_ctx(kb.reshape(b, t, KV_B, HD_B), vb.reshape(b, t, KV_B, HD_B), kb_bf, vb_bf)
        sh1, sc1, g1, sh2, sc2, g2 = jnp.split(c_mod[l], 6, axis=-1)
        h = _adaln(xf, g_norm[l, 0], seq(sh1), seq(sc1), tm=tm, tiles_per_seq=tps)
        if l < N_A_LAYERS:
            both = (jnp.float32, jnp.bfloat16)
            only_bf = (jnp.bfloat16,)
            kvw = KV_A * HD_A
            q = _matmul(h, w_in_a, layer=l, col0=0, ncols=QW_A, tm=tm, tn=512, rope=(tab_a, ROT_A // 2),
                        out_dtypes=only_bf)
            k, k_bf = _matmul(h, w_in_a, layer=l, col0=QW_A, ncols=kvw, tm=tm, tn=kvw,
                              rope=(tab_a, ROT_A // 2), out_dtypes=both)
            v, v_bf = _matmul(h, w_in_a, layer=l, col0=QW_A + kvw, ncols=kvw, tm=tm, tn=kvw, out_dtypes=both)
            qi = _matmul(h, w_in_a, layer=l, col0=QW_A + 2 * kvw, ncols=H_IDX * HD_IDX, tm=tm, tn=512,
                         rope=(tab_i, ROT_IDX // 2), out_dtypes=only_bf)
            w_tail = jnp.pad(w_in_a[l][:, QW_A + 2 * kvw + H_IDX * HD_IDX:],
                             ((0, 0), (0, LANES - HD_IDX - H_IDX)))
            tail = _matmul(h, w_tail, tm=tm, tn=LANES, rope=(tab_t, ROT_IDX // 2))
            o = sparse_attn(l, q, qi, k_bf, v_bf, tail)
            rows_a.append((k.reshape(b, t, KV_A, HD_A), v.reshape(b, t, KV_A, HD_A),
                           tail[:, :HD_IDX].reshape(b, t, HD_IDX)))
            w_o, lo = w_out_a, l
        else:
            lb = l - N_A_LAYERS
            q = _matmul(h, w_q_b, layer=lb, tm=tm, tn=512, rope=(tab_b, ROT_B // 2),
                        out_dtypes=(jnp.bfloat16,))
            o = ctx[0](q, sinks_b[lb])
            w_o, lo = w_out_b, lb
        xf = _matmul(o, w_o, layer=lo, tm=tm, tn=512, res=(xf, seq(g1), tps))
        if l % 2 == 0:
            h = _adaln(xf, g_norm[l, 1], seq(sh2), seq(sc2), tm=tm, tiles_per_seq=tps)
            xf = _ffn(h, w_gate_d, w_up_d, w_down_d, layer=l // 2, tm=min(tm, 512), tf=256,
                      res=(xf, seq(g2), 1 if per_row else t // min(tm, 512)))
        else:
            h, logits = _adaln(xf, g_norm[l, 1], seq(sh2), seq(sc2), tm=tm, tiles_per_seq=tps,
                               router=(w_router[l // 2], b_router[l // 2]))
            f = _moe(h, logits, w_gate_e, w_up_e, w_down_e, layer=l // 2, tm=tm, tf=256)
            g2r = jnp.repeat(g2, t, axis=0)
            xf = xf + g2r * f
    fsh, fsc = jnp.split(f_mod, 2, axis=-1)
    y = _adaln(xf, g_final, seq(fsh), seq(fsc), tm=tm, tiles_per_seq=tps, out_dtype=jnp.float32)
    return y.reshape(b, t, d), rows_a, ctx


def kernel(x_prompt, x_sample, cache_k_a, cache_v_a, cache_kidx_a, state_k_b, state_v_b, page_table,
           c_prompt, c_sample, g_norm, w_mod, b_mod, w_in_a, w_out_a, g_kv, w_kvmod, b_kvmod, w_kv_b,
           w_q_b, sinks_b, w_out_b, w_gate_d, w_up_d, w_down_d, w_router, b_router, w_gate_e, w_up_e,
           w_down_e, g_final, w_fmod, b_fmod):
    n_pages = page_table.shape[1]
    past = n_pages * PAGE_SIZE
    nb_p, nb_s = c_prompt.shape[0], c_sample.shape[0]

    c_rows = 48
    c_all = jnp.zeros((c_rows, D_MODEL), jnp.bfloat16)
    c_all = c_all.at[:nb_p + nb_s].set(jnp.concatenate([c_prompt, c_sample], axis=0).astype(jnp.bfloat16))
    mods = [_matmul(c_all, w_mod, layer=l, tm=c_rows, tn=512, bias=b_mod[l][None]) for l in range(DEPTH)]
    kvm = _matmul(c_all, w_kvmod, tm=c_rows, tn=512, bias=b_kvmod[None])
    fm = _matmul(c_all, w_fmod, tm=c_rows, tn=512, bias=b_fmod[None])

    sb, st = x_sample.shape[:2]
    w_buf = state_k_b.shape[1]
    kvw_b = KV_B * HD_B

    def prompt_sparse_attn(li, q, qi, k_bf, v_bf, tail):
        b, t = x_prompt.shape[:2]
        return _prompt_sparse_attn(q, qi, k_bf, v_bf, tail, b=b, t=t, tq=256, tk=256,
                                   n_sel=min(TOPK_MAX, t // 4))

    def sample_sparse_attn(li, q, qi, k_bf, v_bf, tail):
        r3 = lambda a: a.reshape(sb, st, a.shape[-1])
        return _sample_sparse_attn(
            r3(q), r3(qi), r3(tail[:, HD_IDX:HD_IDX + H_IDX]), r3(tail[:, :HD_IDX]), r3(k_bf), r3(v_bf),
            cache_k_a, cache_v_a, cache_kidx_a, page_table, layer=li, n_sel=min(TOPK_MAX, (past + st) // 4))

    def prompt_ctx(kb, vb, kb_bf, vb_bf):
        t = kb.shape[1]
        wp = min(WINDOW, t)

        def attend(q, sinks):
            return _window_attn(q, kb_bf, kb_bf, vb_bf, vb_bf, sinks, tq=WINDOW, blocks_per_seq=t // WINDOW,
                                mask_first=True)

        return attend, kb[:, t - wp:], vb[:, t - wp:]

    def sample_ctx(kb, vb, kb_bf, vb_bf):
        tq = 16
        pad_new = lambda a: jnp.pad(a.reshape(sb, st, kvw_b), ((0, 0), (0, WINDOW - st), (0, 0))).reshape(-1, kvw_b)
        k_prev = state_k_b.astype(jnp.bfloat16).reshape(sb * w_buf, kvw_b)
        v_prev = state_v_b.astype(jnp.bfloat16).reshape(sb * w_buf, kvw_b)
        k_cur, v_cur = pad_new(kb_bf), pad_new(vb_bf)

        def attend(q, sinks):
            qp = jnp.pad(q.reshape(sb, st, -1), ((0, 0), (0, tq - st), (0, 0))).reshape(sb * tq, -1)
            o = _window_attn(qp, k_prev, k_cur, v_prev, v_cur, sinks, tq=tq, blocks_per_seq=1, mask_first=False)
            return o.reshape(sb, tq, -1)[:, :st].reshape(sb * st, -1)

        k_ctx = jnp.concatenate([state_k_b, kb], axis=1)
        v_ctx = jnp.concatenate([state_v_b, vb], axis=1)
        return attend, k_ctx[:, -w_buf:], v_ctx[:, -w_buf:]

    weights = (g_norm, w_in_a, w_out_a, g_kv, w_kv_b, w_q_b, sinks_b, w_out_b, w_gate_d, w_up_d, w_down_d,
               w_router, b_router, w_gate_e, w_up_e, w_down_e, g_final)
    pos_p = jnp.arange(x_prompt.shape[1])
    pos_s = past + jnp.arange(x_sample.shape[1])
    y_prompt, rows_p, ctx_p = _trunk(
        x_prompt, [mm[:nb_p] for mm in mods], kvm[:nb_p], fm[:nb_p], pos_p, weights,
        tm=1024, sparse_attn=prompt_sparse_attn, make_ctx=prompt_ctx)
    y_sample, rows_s, ctx_s = _trunk(
        x_sample, [mm[nb_p:nb_p + nb_s] for mm in mods], kvm[nb_p:nb_p + nb_s], fm[nb_p:nb_p + nb_s], pos_s,
        weights, tm=x_sample.shape[0] * x_sample.shape[1], sparse_attn=sample_sparse_attn, make_ctx=sample_ctx)

    return (y_prompt, y_sample,
            jnp.stack([r[0] for r in rows_p]), jnp.stack([r[1] for r in rows_p]), jnp.stack([r[2] for r in rows_p]),
            jnp.stack([r[0] for r in rows_s]), jnp.stack([r[1] for r in rows_s]), jnp.stack([r[2] for r in rows_s]),
            ctx_p[1], ctx_p[2], ctx_s[1], ctx_s[2])
```

```python
import functools
import math

import jax
import jax.numpy as jnp
import numpy as np
from jax import lax
from jax.experimental import pallas as pl
from jax.experimental.pallas import tpu as pltpu

D_MODEL = 2048
DEPTH = 4
PAGE_SIZE = 128
HD_A = 128
H_A = D_MODEL // HD_A
KV_A = 4
G_A = H_A // KV_A
QW_A = H_A * HD_A
H_IDX = 16
HD_IDX = 64
TOPK_MAX = 256
IDX_W_SCALE = (H_IDX * HD_IDX) ** -0.5
Q_BLOCK = 128
HD_B = 64
H_B = D_MODEL // HD_B
KV_B = 4
G_B = H_B // KV_B
QW_B = H_B * HD_B
KVW_B = 2 * KV_B * HD_B
WINDOW = 128
ROPE_THETA = 500000.0
ROT_A = HD_A // 4
ROT_IDX = HD_IDX // 4
ROT_B = HD_B // 4
N_EXPERTS = 8
TOP_K_EXPERTS = 2
EPS = 1e-6
N_A_LAYERS = DEPTH // 2

LANES = 128
VMEM_LIMIT = 56 * 1024 * 1024


def _params(*sem):
    return pltpu.CompilerParams(dimension_semantics=sem, vmem_limit_bytes=VMEM_LIMIT)


def _rope_table(pos, head_dim, rot):
    half = rot // 2
    inv = jnp.exp(-math.log(ROPE_THETA) * jnp.arange(half, dtype=jnp.float32) / half)
    ang = pos.astype(jnp.float32)[:, None] * inv[None, :]
    cos, sin = jnp.cos(ang), jnp.sin(ang)
    d = np.arange(LANES) % head_dim
    idx = np.where(d < half, d, np.clip(d - half, 0, half - 1))
    cos_l, sin_l = cos[:, idx], sin[:, idx]
    c = jnp.where((d < rot)[None], cos_l, 1.0)
    u = jnp.where(((d >= half) & (d < rot))[None], sin_l, 0.0)
    dn = jnp.where((d < half)[None], -sin_l, 0.0)
    return jnp.concatenate([c, u, dn], axis=1)


def _tail_table(pos):
    t = _rope_table(pos, HD_IDX, ROT_IDX)
    lane = np.arange(LANES)
    c = jnp.where((lane < HD_IDX)[None], t[:, :LANES],
                  jnp.where((lane < HD_IDX + H_IDX)[None], IDX_W_SCALE, 1.0))
    keep = (lane < HD_IDX)[None]
    u = jnp.where(keep, t[:, LANES:2 * LANES], 0.0)
    dn = jnp.where(keep, t[:, 2 * LANES:], 0.0)
    return jnp.concatenate([c, u, dn], axis=1)


def _adaln_kernel(*refs, has_router):
    x_ref, g_ref, sh_ref, sc_ref = refs[:4]
    x = x_ref[...]
    y = x * lax.rsqrt(jnp.mean(x * x, axis=-1, keepdims=True) + EPS)
    h = (y * g_ref[...]) * (1.0 + sc_ref[0]) + sh_ref[0]
    if has_router:
        wr_ref, br_ref, h_ref, lg_ref = refs[4:]
        h_ref[...] = h.astype(h_ref.dtype)
        lg_ref[...] = jnp.dot(h.astype(jnp.bfloat16), wr_ref[...], preferred_element_type=jnp.float32) + br_ref[...]
    else:
        h_ref = refs[4]
        h_ref[...] = h.astype(h_ref.dtype)


def _adaln(x, g, shift, scale, *, tm, tiles_per_seq, out_dtype=jnp.bfloat16, router=None):
    m, d = x.shape
    r = shift.shape[1]
    in_specs = [
        pl.BlockSpec((tm, d), lambda i: (i, 0)),
        pl.BlockSpec((1, d), lambda i: (0, 0)),
        pl.BlockSpec((1, r, d), lambda i: (i // tiles_per_seq, 0, 0)),
        pl.BlockSpec((1, r, d), lambda i: (i // tiles_per_seq, 0, 0)),
    ]
    args = [x, g.reshape(1, d), shift, scale]
    out_shape = [jax.ShapeDtypeStruct((m, d), out_dtype)]
    out_specs = [pl.BlockSpec((tm, d), lambda i: (i, 0))]
    if router is not None:
        w_r, b_r = router
        wr = jnp.zeros((d, LANES), jnp.bfloat16).at[:, :N_EXPERTS].set(w_r.astype(jnp.bfloat16))
        br = jnp.zeros((1, LANES), jnp.float32).at[0, :N_EXPERTS].set(b_r.astype(jnp.float32))
        in_specs += [pl.BlockSpec((d, LANES), lambda i: (0, 0)), pl.BlockSpec((1, LANES), lambda i: (0, 0))]
        args += [wr, br]
        out_shape.append(jax.ShapeDtypeStruct((m, LANES), jnp.float32))
        out_specs.append(pl.BlockSpec((tm, LANES), lambda i: (i, 0)))
    outs = pl.pallas_call(
        functools.partial(_adaln_kernel, has_router=router is not None),
        grid=(m // tm,),
        in_specs=in_specs,
        out_specs=out_specs,
        out_shape=out_shape,
        compiler_params=_params("parallel"),
        name="adaln",
    )(*args)
    return outs if router is not None else outs[0]


def _mm_kernel(*refs, tn, has_bias, rope_half, has_res):
    it = iter(refs)
    a_ref = next(it)
    w_ref = next(it)
    bias_ref = next(it) if has_bias else None
    tab_ref = next(it) if rope_half else None
    res_ref = next(it) if has_res else None
    gate_ref = next(it) if has_res else None
    out_refs = list(it)
    acc = jnp.dot(a_ref[...], w_ref[...].astype(jnp.bfloat16), preferred_element_type=jnp.float32)
    if has_bias:
        acc = acc + bias_ref[...]
    if rope_half:
        c = tab_ref[:, :LANES]
        u = tab_ref[:, LANES:2 * LANES]
        dn = tab_ref[:, 2 * LANES:]
        parts = []
        for grp in range(tn // LANES):
            xg = acc[:, grp * LANES:(grp + 1) * LANES]
            parts.append(xg * c + pltpu.roll(xg, rope_half, 1) * u + pltpu.roll(xg, LANES - rope_half, 1) * dn)
        acc = parts[0] if len(parts) == 1 else jnp.concatenate(parts, axis=1)
    if has_res:
        acc = res_ref[...] + gate_ref[0] * acc
    for o in out_refs:
        o[...] = acc.astype(o.dtype)


def _matmul(a, w, *, layer=0, col0=0, ncols=None, tm, tn, bias=None, rope=None, res=None,
            out_dtypes=(jnp.float32,)):
    if w.ndim == 2:
        w = w[None]
    m, k = a.shape
    ncols = w.shape[2] - col0 if ncols is None else ncols
    assert ncols % tn == 0 and col0 % tn == 0 and m % tm == 0
    cb = col0 // tn
    in_specs = [
        pl.BlockSpec((tm, k), lambda i, j: (i, 0)),
        pl.BlockSpec((None, k, tn), lambda i, j: (layer, 0, cb + j)),
    ]
    args = [a, w]
    if bias is not None:
        in_specs.append(pl.BlockSpec((1, tn), lambda i, j: (0, cb + j)))
        args.append(bias)
    rope_half = 0
    if rope is not None:
        tab, rope_half = rope
        tab_tiles = tab.shape[0] // tm
        in_specs.append(pl.BlockSpec((tm, 3 * LANES), lambda i, j: (i % tab_tiles, 0)))
        args.append(tab)
    if res is not None:
        resid, gate, tps = res
        r = gate.shape[1]
        in_specs.append(pl.BlockSpec((tm, tn), lambda i, j: (i, j)))
        in_specs.append(pl.BlockSpec((1, r, tn), lambda i, j: (i // tps, 0, j)))
        args += [resid, gate]
    outs = pl.pallas_call(
        functools.partial(_mm_kernel, tn=tn, has_bias=bias is not None, rope_half=rope_half,
                          has_res=res is not None),
        grid=(m // tm, ncols // tn),
        in_specs=in_specs,
        out_specs=[pl.BlockSpec((tm, tn), lambda i, j: (i, j)) for _ in out_dtypes],
        out_shape=[jax.ShapeDtypeStruct((m, ncols), dt) for dt in out_dtypes],
        compiler_params=_params("parallel", "arbitrary"),
        name="matmul",
    )(*args)
    return outs if len(outs) > 1 else outs[0]


def _gather_rows(idx_ref, base, src_hbm, dst, sem, n_rows):
    def issue(r, carry):
        pltpu.make_async_copy(src_hbm.at[pl.ds(idx_ref[base + r], 1)], dst.at[pl.ds(r, 1)], sem).start()
        return carry

    lax.fori_loop(0, n_rows, issue, 0, unroll=8)
    pltpu.make_async_copy(src_hbm.at[pl.ds(0, n_rows)], dst, sem).wait()


def _ffn_kernel(te_ref, tv_ref, rt_ref, *refs, nj, mode, tm):
    del te_ref
    if mode == "dense":
        x_ref, wg_ref, wu_ref, wd_ref, res_ref, gate_ref, o_ref = refs
    else:
        x_hbm, wg_ref, wu_ref, wd_ref, rs_ref, o_ref, xf_scr, x_ref, sem = refs
    i = pl.program_id(0)
    j = pl.program_id(1)

    @pl.when(j == 0)
    def _():
        o_ref[...] = jnp.zeros_like(o_ref)

    if mode == "routed":
        @pl.when((j == 0) & (tv_ref[i] > 0))
        def _():
            _gather_rows(rt_ref, i * tm, x_hbm, xf_scr, sem, tm)
            x_ref[...] = xf_scr[...].astype(x_ref.dtype)

    @pl.when(tv_ref[i] > 0)
    def _():
        x = x_ref[...]
        g = jnp.dot(x, wg_ref[...].astype(jnp.bfloat16), preferred_element_type=jnp.float32)
        u = jnp.dot(x, wu_ref[...].astype(jnp.bfloat16), preferred_element_type=jnp.float32)
        act = (g * (1.0 / (1.0 + jnp.exp(-g)))) * u
        o_ref[...] += jnp.dot(act.astype(jnp.bfloat16), wd_ref[...].astype(jnp.bfloat16),
                              preferred_element_type=jnp.float32)

    @pl.when(j == nj - 1)
    def _():
        if mode == "dense":
            o_ref[...] = res_ref[...] + gate_ref[0] * o_ref[...]
        else:
            o_ref[...] = rs_ref[...] * o_ref[...]


def _ffn(x, wg, wu, wd, *, layer, tm, tf, tile_expert=None, tile_valid=None, row_token=None, res=None,
         row_scale=None):
    d = x.shape[1]
    f = wg.shape[-1]
    dense = res is not None
    m = x.shape[0] if dense else row_token.shape[0]
    nt, nj = m // tm, f // tf
    if dense:
        tile_expert = jnp.full((nt,), layer, jnp.int32)
        tile_valid = jnp.ones((nt,), jnp.int32)
        row_token = jnp.zeros((1,), jnp.int32)
    else:
        tile_expert = tile_expert + layer * wg.shape[1]
        wg, wu, wd = wg.reshape(-1, d, f), wu.reshape(-1, d, f), wd.reshape(-1, f, d)

    def jj(i, j, tv):
        return jnp.where(tv[i] > 0, j, nj - 1)

    in_specs = [
        pl.BlockSpec((tm, d), lambda i, j, te, tv, rt: (i, 0)) if dense else pl.BlockSpec(memory_space=pl.ANY),
        pl.BlockSpec((None, d, tf), lambda i, j, te, tv, rt: (te[i], 0, jj(i, j, tv))),
        pl.BlockSpec((None, d, tf), lambda i, j, te, tv, rt: (te[i], 0, jj(i, j, tv))),
        pl.BlockSpec((None, tf, d), lambda i, j, te, tv, rt: (te[i], jj(i, j, tv), 0)),
    ]
    args = [x, wg, wu, wd]
    scratch = []
    if dense:
        resid, gate, tps = res
        r = gate.shape[1]
        in_specs.append(pl.BlockSpec((tm, d), lambda i, j, te, tv, rt: (i, 0)))
        in_specs.append(pl.BlockSpec((1, r, d), lambda i, j, te, tv, rt: (i // tps, 0, 0)))
        args += [resid, gate]
    else:
        in_specs.append(pl.BlockSpec((tm, 1), lambda i, j, te, tv, rt: (i, 0)))
        args.append(row_scale)
        scratch = [pltpu.VMEM((tm, d), jnp.float32), pltpu.VMEM((tm, d), jnp.bfloat16),
                   pltpu.SemaphoreType.DMA(())]
    return pl.pallas_call(
        functools.partial(_ffn_kernel, nj=nj, mode="dense" if dense else "routed", tm=tm),
        grid_spec=pltpu.PrefetchScalarGridSpec(
            num_scalar_prefetch=3,
            grid=(nt, nj),
            in_specs=in_specs,
            out_specs=pl.BlockSpec((tm, d), lambda i, j, te, tv, rt: (i, 0)),
            scratch_shapes=scratch,
        ),
        out_shape=jax.ShapeDtypeStruct((m, d), jnp.float32),
        compiler_params=_params("parallel", "arbitrary"),
        name="ffn_dense" if dense else "ffn_routed",
    )(tile_expert, tile_valid, row_token, *args)


def _combine_kernel(dest_ref, ys_hbm, res_ref, gate_ref, o_ref, buf, sem, *, tm):
    i = pl.program_id(0)
    npair_half = pl.num_programs(0) * tm

    def issue(r, carry):
        for k in range(TOP_K_EXPERTS):
            row = dest_ref[k * npair_half + i * tm + r]
            pltpu.make_async_copy(ys_hbm.at[pl.ds(row, 1)], buf.at[pl.ds(k * tm + r, 1)], sem).start()
        return carry

    lax.fori_loop(0, tm, issue, 0, unroll=4)
    pltpu.make_async_copy(ys_hbm.at[pl.ds(0, TOP_K_EXPERTS * tm)], buf, sem).wait()
    f = buf[0:tm]
    for k in range(1, TOP_K_EXPERTS):
        f = f + buf[k * tm:(k + 1) * tm]
    o_ref[...] = res_ref[...] + gate_ref[0] * f


def _moe_combine(ys, dest, res, gate, *, tm, tiles_per_seq):
    n, d = res.shape
    r = gate.shape[1]
    dest_km = dest.reshape(n, TOP_K_EXPERTS).T.reshape(-1)
    return pl.pallas_call(
        functools.partial(_combine_kernel, tm=tm),
        grid_spec=pltpu.PrefetchScalarGridSpec(
            num_scalar_prefetch=1,
            grid=(n // tm,),
            in_specs=[
                pl.BlockSpec(memory_space=pl.ANY),
                pl.BlockSpec((tm, d), lambda i, dr: (i, 0)),
                pl.BlockSpec((1, r, d), lambda i, dr: (i // tiles_per_seq, 0, 0)),
            ],
            out_specs=pl.BlockSpec((tm, d), lambda i, dr: (i, 0)),
            scratch_shapes=[pltpu.VMEM((TOP_K_EXPERTS * tm, d), jnp.float32), pltpu.SemaphoreType.DMA(())],
        ),
        out_shape=jax.ShapeDtypeStruct((n, d), jnp.float32),
        compiler_params=_params("arbitrary"),
        name="moe_combine",
    )(dest_km, ys, res, gate)


def _moe(h, logits, wg, wu, wd, *, layer, tm, tf, res):
    n, d = h.shape
    vals, idx = lax.top_k(logits[:, :N_EXPERTS], TOP_K_EXPERTS)
    gates = jax.nn.softmax(vals, axis=-1)
    npair = n * TOP_K_EXPERTS
    nt = npair // tm + N_EXPERTS
    flat_e = idx.reshape(-1).astype(jnp.int32)
    order = jnp.argsort(flat_e, stable=True)
    sorted_e = flat_e[order]
    counts = jnp.sum(flat_e[:, None] == jnp.arange(N_EXPERTS)[None, :], axis=0).astype(jnp.int32)
    tiles_e = (counts + tm - 1) // tm
    tile_end = jnp.cumsum(tiles_e)
    tile_start = tile_end - tiles_e
    grp_start = jnp.cumsum(counts) - counts
    rank = jnp.arange(npair, dtype=jnp.int32) - grp_start[sorted_e]
    dest_sorted = tile_start[sorted_e] * tm + rank
    dest = jnp.zeros((npair,), jnp.int32).at[order].set(dest_sorted)
    rows = nt * tm
    row_token = jnp.zeros((rows,), jnp.int32).at[dest].set(jnp.arange(npair, dtype=jnp.int32) // TOP_K_EXPERTS)
    row_gate = jnp.zeros((rows,), jnp.float32).at[dest].set(gates.reshape(-1))
    tile_ids = jnp.arange(nt, dtype=jnp.int32)
    total = tile_end[-1]
    tile_valid = (tile_ids < total).astype(jnp.int32)
    last_e = jnp.searchsorted(tile_end, total - 1, side="right").astype(jnp.int32)
    tile_expert = jnp.where(tile_valid > 0, jnp.searchsorted(tile_end, tile_ids, side="right").astype(jnp.int32),
                            last_e)
    tile_expert = jnp.clip(tile_expert, 0, N_EXPERTS - 1)
    ys = _ffn(h, wg, wu, wd, layer=layer, tm=tm, tf=tf, tile_expert=tile_expert, tile_valid=tile_valid,
              row_token=row_token, row_scale=row_gate[:, None])
    resid, gate, tmc, tps = res
    return _moe_combine(ys, dest, resid, gate, tm=tmc, tiles_per_seq=tps)


INT_MIN = -(2 ** 31)
NEG = -1e30


def _sortable(score):
    bits = pltpu.bitcast(score, jnp.int32)
    return bits ^ ((bits >> 31) & jnp.int32(0x7FFFFFFF))


def _kth_largest(count_ge, shape, n_sel):
    def step(s, t):
        cand = t + lax.shift_left(jnp.int32(1), 31 - s)
        return jnp.where(count_ge(cand) >= n_sel, cand, t)

    return lax.fori_loop(0, 32, step, jnp.full(shape, INT_MIN, jnp.int32))


def _psat_kernel(q_ref, qi_ref, k_ref, vt_ref, tail_ref, wt_ref, o_ref, key_scr, m_scr, l_scr, acc_scr, *,
                 tq, tk, n_sel):
    i = pl.program_id(1)
    nck = (i + 1) * (tq // tk)
    row0 = i * tq
    nt = (((1,), (1,)), ((), ()))
    lane = lax.broadcasted_iota(jnp.int32, (tk, LANES), 1)
    kidx0 = lax.broadcasted_iota(jnp.int32, (tk, tq), 0)
    qidx = row0 + lax.broadcasted_iota(jnp.int32, (tk, tq), 1)

    def score_chunk(c, carry):
        ks = pl.multiple_of(c * tk, tk)
        kf = jnp.where(lane < HD_IDX, tail_ref[pl.ds(ks, tk), :], 0.0)
        k_even = kf.astype(jnp.bfloat16)
        k_odd = pltpu.roll(kf, HD_IDX, 1).astype(jnp.bfloat16)
        sc = jnp.zeros((tk, tq), jnp.float32)
        for h in range(H_IDX):
            qp = qi_ref[:, (h // 2) * LANES:(h // 2 + 1) * LANES]
            d = lax.dot_general(k_even if h % 2 == 0 else k_odd, qp, nt, preferred_element_type=jnp.float32)
            sc = sc + jnp.maximum(d, 0.0) * wt_ref[h:h + 1, :]
        key_scr[c] = jnp.where(ks + kidx0 <= qidx, _sortable(sc), INT_MIN)
        return carry

    lax.fori_loop(0, nck, score_chunk, 0)

    def count_ge(cand):
        def chunk(c, acc):
            hit = jnp.where(key_scr[c] >= cand, 1.0, 0.0)
            return acc + jnp.sum(hit.reshape(tk // 8, 8, tq), axis=0)

        acc = lax.fori_loop(0, nck, chunk, jnp.zeros((8, tq), jnp.float32))
        return jnp.sum(acc, axis=0, keepdims=True)

    thr = jnp.maximum(_kth_largest(count_ge, (1, tq), float(n_sel)), INT_MIN + 1)

    m_scr[...] = jnp.full_like(m_scr, NEG)
    l_scr[...] = jnp.zeros_like(l_scr)
    acc_scr[...] = jnp.zeros_like(acc_scr)
    scale = HD_A ** -0.5

    def att_chunk(c, carry):
        ks = pl.multiple_of(c * tk, tk)
        bias = jnp.where(key_scr[c] >= thr, 0.0, NEG)
        bias = jnp.concatenate([bias] * G_A, axis=1)
        for g in range(KV_A):
            qg = jnp.concatenate(
                [q_ref[:, (g * G_A + j) * HD_A:(g * G_A + j + 1) * HD_A] for j in range(G_A)], axis=0)
            kg = k_ref[pl.ds(ks, tk), g * HD_A:(g + 1) * HD_A]
            s = lax.dot_general(kg, qg, nt, preferred_element_type=jnp.float32) * scale + bias
            m_old = m_scr[g]
            m_new = jnp.maximum(m_old, jnp.max(s, axis=0, keepdims=True))
            alpha = jnp.exp(m_old - m_new)
            p = jnp.exp(s - m_new)
            l_scr[g] = alpha * l_scr[g] + jnp.sum(p, axis=0, keepdims=True)
            vtg = vt_ref[c, g * HD_A:(g + 1) * HD_A, :]
            acc_scr[g] = alpha * acc_scr[g] + jnp.dot(vtg, p.astype(jnp.bfloat16),
                                                      preferred_element_type=jnp.float32)
            m_scr[g] = m_new
        return carry

    lax.fori_loop(0, nck, att_chunk, 0)

    for g in range(KV_A):
        og = acc_scr[g] * (1.0 / l_scr[g])
        for j in range(G_A):
            h = g * G_A + j
            o_ref[:, h * HD_A:(h + 1) * HD_A] = og[:, j * tq:(j + 1) * tq].T.astype(o_ref.dtype)


def _prompt_sparse_attn_t(q, qi, k, v, tail, *, b, t, tq, tk, n_sel):
    nqt = t // tq
    nck = t // tk
    kvw = v.shape[1]
    vt = v.reshape(b, nck, tk, kvw).transpose(0, 1, 3, 2)
    wt = tail[:, HD_IDX:HD_IDX + H_IDX].T
    return pl.pallas_call(
        functools.partial(_psat_kernel, tq=tq, tk=tk, n_sel=n_sel),
        grid=(b, nqt),
        in_specs=[
            pl.BlockSpec((tq, q.shape[1]), lambda bb, i: (bb * nqt + i, 0)),
            pl.BlockSpec((tq, qi.shape[1]), lambda bb, i: (bb * nqt + i, 0)),
            pl.BlockSpec((t, k.shape[1]), lambda bb, i: (bb, 0)),
            pl.BlockSpec((None, nck, kvw, tk), lambda bb, i: (bb, 0, 0, 0)),
            pl.BlockSpec((t, LANES), lambda bb, i: (bb, 0)),
            pl.BlockSpec((H_IDX, tq), lambda bb, i: (0, bb * nqt + i)),
        ],
        out_specs=pl.BlockSpec((tq, q.shape[1]), lambda bb, i: (bb * nqt + i, 0)),
        out_shape=jax.ShapeDtypeStruct(q.shape, jnp.bfloat16),
        scratch_shapes=[
            pltpu.VMEM((nck, tk, tq), jnp.int32),
            pltpu.VMEM((KV_A, 1, G_A * tq), jnp.float32),
            pltpu.VMEM((KV_A, 1, G_A * tq), jnp.float32),
            pltpu.VMEM((KV_A, HD_A, G_A * tq), jnp.float32),
        ],
        compiler_params=_params("parallel", "arbitrary"),
        name="prompt_sparse_attn_t",
    )(q, qi, k, vt, tail, wt)


QPAD = 8
PAGES_PER_STEP = 16
PAGES_PER_INDEX_STEP = 32


def _ssi_kernel(pt_ref, qi_ref, w_ref, kn_ref, *refs, pp, n_sel, s_new, ns):
    del pt_ref
    page_refs = refs[:pp]
    bias_ref, bias_new_ref, key_scr = refs[pp:]
    s = pl.program_id(1)
    n = pp * PAGE_SIZE

    def scores(kt):
        d = jnp.dot(qi_ref[...], kt, preferred_element_type=jnp.float32)
        sc = jnp.maximum(d, 0.0) * w_ref[...]
        return jnp.sum(sc.reshape(QPAD, H_IDX, kt.shape[1]), axis=1)

    kt = jnp.concatenate([r[...] for r in page_refs], axis=1).astype(jnp.bfloat16)
    qrow = lax.broadcasted_iota(jnp.int32, (QPAD, n), 0)
    key_scr[s] = jnp.where(qrow < s_new, _sortable(scores(kt)), INT_MIN)

    @pl.when(s == ns - 1)
    def _():
        qr = lax.broadcasted_iota(jnp.int32, (QPAD, PAGE_SIZE), 0)
        col = lax.broadcasted_iota(jnp.int32, (QPAD, PAGE_SIZE), 1)
        key_new = jnp.where((col <= qr) & (qr < s_new), _sortable(scores(kn_ref[...])), INT_MIN)

        def count_ge(cand):
            acc = jnp.where(key_new >= cand, 1.0, 0.0)
            for c in range(ns):
                hit = jnp.where(key_scr[c] >= cand, 1.0, 0.0)
                for u in range(n // LANES):
                    acc = acc + hit[:, u * LANES:(u + 1) * LANES]
            return jnp.sum(acc, axis=1, keepdims=True)

        thr = jnp.maximum(_kth_largest(count_ge, (QPAD, 1), float(n_sel)), INT_MIN + 1)
        for c in range(ns):
            bias_ref[:, c * n:(c + 1) * n] = jnp.where(key_scr[c] >= thr, 0.0, NEG)
        bias_new_ref[...] = jnp.where(key_new >= thr, 0.0, NEG)


def _ssa_kernel(pt_ref, q_ref, bias_ref, bias_new_ref, kn_ref, vn_ref, *refs, pp, ns):
    del pt_ref
    k_refs = refs[:pp]
    v_refs = refs[pp:2 * pp]
    o_ref, m_scr, l_scr, acc_scr = refs[2 * pp:]
    s = pl.program_id(1)
    nt = (((1,), (1,)), ((), ()))
    scale = HD_A ** -0.5

    @pl.when(s == 0)
    def _():
        m_scr[...] = jnp.full_like(m_scr, NEG)
        l_scr[...] = jnp.zeros_like(l_scr)
        acc_scr[...] = jnp.zeros_like(acc_scr)

    rows_g = G_A * QPAD

    def update(kg, vg, bias):
        bias = jnp.concatenate([bias] * (KV_A * G_A), axis=0)
        sc = jnp.concatenate(
            [lax.dot_general(q_ref[g], kg[g], nt, preferred_element_type=jnp.float32) for g in range(KV_A)],
            axis=0) * scale + bias
        m_old = m_scr[...]
        m_new = jnp.maximum(m_old, jnp.max(sc, axis=1, keepdims=True))
        alpha = jnp.exp(m_old - m_new)
        p = jnp.exp(sc - m_new)
        l_scr[...] = alpha * l_scr[...] + jnp.sum(p, axis=1, keepdims=True)
        pb = p.astype(jnp.bfloat16)
        pv = jnp.concatenate(
            [jnp.dot(pb[g * rows_g:(g + 1) * rows_g], vg[g], preferred_element_type=jnp.float32)
             for g in range(KV_A)], axis=0)
        acc_scr[...] = alpha * acc_scr[...] + pv
        m_scr[...] = m_new

    def head_rows(page_refs, g):
        return jnp.concatenate([r[pl.ds(g, PAGE_SIZE, stride=KV_A), :] for r in page_refs],
                               axis=0).astype(jnp.bfloat16)

    update([head_rows(k_refs, g) for g in range(KV_A)], [head_rows(v_refs, g) for g in range(KV_A)],
           bias_ref[...])

    @pl.when(s == ns - 1)
    def _():
        kn, vn = kn_ref[...], vn_ref[...]
        update([kn[:, g * HD_A:(g + 1) * HD_A] for g in range(KV_A)],
               [vn[:, g * HD_A:(g + 1) * HD_A] for g in range(KV_A)], bias_new_ref[...])
        o_ref[...] = acc_scr[...] / l_scr[...]


def _sample_sparse_attn(q, qi, wi, ki, k, v, cache_k, cache_v, cache_kidx, page_table, *, layer, n_sel):
    db, s_new = q.shape[:2]
    n_pages = page_table.shape[1]
    ppi, ppa = min(PAGES_PER_INDEX_STEP, n_pages), min(PAGES_PER_STEP, n_pages)
    nsi, nsa = n_pages // ppi, n_pages // ppa
    kvw = KV_A * HD_A
    rows = KV_A * G_A * QPAD
    pad_q = ((0, 0), (0, QPAD - s_new), (0, 0), (0, 0))
    pad_k = ((0, 0), (0, PAGE_SIZE - s_new), (0, 0))
    qi_r = jnp.pad(qi.reshape(db, s_new, H_IDX, HD_IDX), pad_q).reshape(db, QPAD * H_IDX, HD_IDX)
    w_r = jnp.pad(wi, ((0, 0), (0, QPAD - s_new), (0, 0))).reshape(db, QPAD * H_IDX, 1)
    ki_n = jnp.swapaxes(jnp.pad(ki.astype(jnp.bfloat16), pad_k), 1, 2)
    k_n = jnp.pad(k, pad_k)
    v_n = jnp.pad(v, pad_k)
    q_r = jnp.pad(q.reshape(db, s_new, KV_A, G_A, HD_A).transpose(0, 2, 3, 1, 4),
                  ((0, 0), (0, 0), (0, 0), (0, QPAD - s_new), (0, 0))).reshape(db, KV_A, G_A * QPAD, HD_A)

    def page_spec(rows_, pp, j):
        return pl.BlockSpec((None, None, rows_, LANES), lambda b, s, pt: (layer, pt[b, s * pp + j], 0, 0))

    cki = jnp.swapaxes(cache_kidx, 2, 3)
    ck = cache_k.reshape(cache_k.shape[:2] + (PAGE_SIZE * KV_A, HD_A))
    cv = cache_v.reshape(cache_v.shape[:2] + (PAGE_SIZE * KV_A, HD_A))

    bias, bias_new = pl.pallas_call(
        functools.partial(_ssi_kernel, pp=ppi, n_sel=n_sel, s_new=s_new, ns=nsi),
        grid_spec=pltpu.PrefetchScalarGridSpec(
            num_scalar_prefetch=1,
            grid=(db, nsi),
            in_specs=[
                pl.BlockSpec((None, QPAD * H_IDX, HD_IDX), lambda b, s, pt: (b, 0, 0)),
                pl.BlockSpec((None, QPAD * H_IDX, 1), lambda b, s, pt: (b, 0, 0)),
                pl.BlockSpec((None, HD_IDX, PAGE_SIZE), lambda b, s, pt: (b, 0, 0)),
            ] + [page_spec(HD_IDX, ppi, j) for j in range(ppi)],
            out_specs=[
                pl.BlockSpec((None, QPAD, n_pages * PAGE_SIZE), lambda b, s, pt: (b, 0, 0)),
                pl.BlockSpec((None, QPAD, PAGE_SIZE), lambda b, s, pt: (b, 0, 0)),
            ],
            scratch_shapes=[pltpu.VMEM((nsi, QPAD, ppi * PAGE_SIZE), jnp.int32)],
        ),
        out_shape=[jax.ShapeDtypeStruct((db, QPAD, n_pages * PAGE_SIZE), jnp.float32),
                   jax.ShapeDtypeStruct((db, QPAD, PAGE_SIZE), jnp.float32)],
        compiler_params=_params("parallel", "arbitrary"),
        name="sample_indexer",
    )(page_table, qi_r, w_r, ki_n, *([cki] * ppi))

    o = pl.pallas_call(
        functools.partial(_ssa_kernel, pp=ppa, ns=nsa),
        grid_spec=pltpu.PrefetchScalarGridSpec(
            num_scalar_prefetch=1,
            grid=(db, nsa),
            in_specs=[
                pl.BlockSpec((None, KV_A, G_A * QPAD, HD_A), lambda b, s, pt: (b, 0, 0, 0)),
                pl.BlockSpec((None, QPAD, ppa * PAGE_SIZE), lambda b, s, pt: (b, 0, s)),
                pl.BlockSpec((None, QPAD, PAGE_SIZE), lambda b, s, pt: (b, 0, 0)),
                pl.BlockSpec((None, PAGE_SIZE, kvw), lambda b, s, pt: (b, 0, 0)),
                pl.BlockSpec((None, PAGE_SIZE, kvw), lambda b, s, pt: (b, 0, 0)),
            ] + [page_spec(ck.shape[2], ppa, j) for j in range(ppa)] * 2,
            out_specs=pl.BlockSpec((None, rows, HD_A), lambda b, s, pt: (b, 0, 0)),
            scratch_shapes=[
                pltpu.VMEM((rows, 1), jnp.float32),
                pltpu.VMEM((rows, 1), jnp.float32),
                pltpu.VMEM((rows, HD_A), jnp.float32),
            ],
        ),
        out_shape=jax.ShapeDtypeStruct((db, rows, HD_A), jnp.float32),
        compiler_params=_params("parallel", "arbitrary"),
        name="sample_sparse_attn",
    )(page_table, q_r, bias, bias_new, k_n, v_n, *([ck] * ppa), *([cv] * ppa))
    o = o.reshape(db, KV_A, G_A, QPAD, HD_A)[:, :, :, :s_new].transpose(0, 3, 1, 2, 4)
    return o.reshape(db * s_new, H_A * HD_A).astype(jnp.bfloat16)


def _wa_kernel(q_ref, kp_ref, kc_ref, vp_ref, vc_ref, sink_ref, o_ref, *, tq, blocks_per_seq, mask_first):
    n = pl.program_id(0)
    nt = (((1,), (1,)), ((), ()))
    r = lax.broadcasted_iota(jnp.int32, (tq, 2 * WINDOW), 0)
    c = lax.broadcasted_iota(jnp.int32, (tq, 2 * WINDOW), 1)
    ok = (c - r >= 1) & (c - r <= WINDOW)
    if mask_first:
        ok = ok & ((n % blocks_per_seq != 0) | (c >= WINDOW))
    bias = jnp.where(ok, 0.0, NEG)
    bias = jnp.concatenate([bias] * G_B, axis=0)
    scale = HD_B ** -0.5
    for g in range(KV_B):
        cols = slice(g * HD_B, (g + 1) * HD_B)
        kcat = jnp.concatenate([kp_ref[:, cols], kc_ref[:, cols]], axis=0)
        vcat = jnp.concatenate([vp_ref[:, cols], vc_ref[:, cols]], axis=0)
        qg = jnp.concatenate(
            [q_ref[:, (g * G_B + j) * HD_B:(g * G_B + j + 1) * HD_B] for j in range(G_B)], axis=0)
        s = lax.dot_general(qg, kcat, nt, preferred_element_type=jnp.float32) * scale + bias
        sink = sink_ref[g]
        m = jnp.maximum(jnp.max(s, axis=1, keepdims=True), sink)
        p = jnp.exp(s - m)
        p = p / (jnp.sum(p, axis=1, keepdims=True) + jnp.exp(sink - m))
        og = jnp.dot(p.astype(jnp.bfloat16), vcat, preferred_element_type=jnp.float32)
        for j in range(G_B):
            h = g * G_B + j
            o_ref[:, h * HD_B:(h + 1) * HD_B] = og[j * tq:(j + 1) * tq].astype(o_ref.dtype)


def _wat_kernel(q_ref, kp_ref, kc_ref, vtp_ref, vtc_ref, sink_ref, o_ref, *, blocks_per_seq):
    tq = WINDOW
    n = pl.program_id(0)
    nt = (((1,), (1,)), ((), ()))
    kidx = lax.broadcasted_iota(jnp.int32, (2 * WINDOW, tq), 0)
    qidx = lax.broadcasted_iota(jnp.int32, (2 * WINDOW, tq), 1)
    ok = (kidx - qidx >= 1) & (kidx - qidx <= WINDOW) & ((n % blocks_per_seq != 0) | (kidx >= WINDOW))
    bias = jnp.where(ok, 0.0, NEG)
    bias = jnp.concatenate([bias] * G_B, axis=1)
    scale = HD_B ** -0.5
    for g in range(KV_B):
        cols = slice(g * HD_B, (g + 1) * HD_B)
        kcat = jnp.concatenate([kp_ref[:, cols], kc_ref[:, cols]], axis=0)
        vcat_t = jnp.concatenate([vtp_ref[cols, :], vtc_ref[cols, :]], axis=1)
        qg = jnp.concatenate(
            [q_ref[:, (g * G_B + j) * HD_B:(g * G_B + j + 1) * HD_B] for j in range(G_B)], axis=0)
        s = lax.dot_general(kcat, qg, nt, preferred_element_type=jnp.float32) * scale + bias
        sink = sink_ref[g]
        m = jnp.maximum(jnp.max(s, axis=0, keepdims=True), sink)
        p = jnp.exp(s - m)
        inv = 1.0 / (jnp.sum(p, axis=0, keepdims=True) + jnp.exp(sink - m))
        ot = jnp.dot(vcat_t, (p * inv).astype(jnp.bfloat16), preferred_element_type=jnp.float32)
        for j in range(0, G_B, 2):
            pair = jnp.concatenate([ot[:, j * tq:(j + 1) * tq], ot[:, (j + 1) * tq:(j + 2) * tq]], axis=0)
            h = g * G_B + j
            o_ref[:, h * HD_B:(h + 2) * HD_B] = pair.T.astype(o_ref.dtype)


def _window_attn_t(q, k, v_t, sinks, *, blocks_per_seq):
    tq = WINDOW
    nblk = q.shape[0] // tq
    kvw = KV_B * HD_B
    sink_rows = jnp.repeat(sinks.astype(jnp.float32).reshape(KV_B, G_B), tq, axis=1)[:, None, :]
    return pl.pallas_call(
        functools.partial(_wat_kernel, blocks_per_seq=blocks_per_seq),
        grid=(nblk,),
        in_specs=[
            pl.BlockSpec((tq, q.shape[1]), lambda n: (n, 0)),
            pl.BlockSpec((WINDOW, kvw), lambda n: (jnp.maximum(n - 1, 0), 0)),
            pl.BlockSpec((WINDOW, kvw), lambda n: (n, 0)),
            pl.BlockSpec((kvw, WINDOW), lambda n: (0, jnp.maximum(n - 1, 0))),
            pl.BlockSpec((kvw, WINDOW), lambda n: (0, n)),
            pl.BlockSpec((KV_B, 1, G_B * tq), lambda n: (0, 0, 0)),
        ],
        out_specs=pl.BlockSpec((tq, q.shape[1]), lambda n: (n, 0)),
        out_shape=jax.ShapeDtypeStruct(q.shape, jnp.bfloat16),
        compiler_params=_params("parallel"),
        name="window_attn_t",
    )(q, k, k, v_t, v_t, sink_rows)


def _window_attn(q, k_prev, k_cur, v_prev, v_cur, sinks, *, tq, blocks_per_seq, mask_first):
    nblk = q.shape[0] // tq
    kvw = KV_B * HD_B
    if mask_first:
        prev = lambda n: (jnp.maximum(n - 1, 0), 0)
    else:
        prev = lambda n: (n, 0)
    cur = lambda n: (n, 0)
    sink_rows = jnp.repeat(sinks.astype(jnp.float32).reshape(KV_B, G_B), tq, axis=1)[:, :, None]
    return pl.pallas_call(
        functools.partial(_wa_kernel, tq=tq, blocks_per_seq=blocks_per_seq, mask_first=mask_first),
        grid=(nblk,),
        in_specs=[
            pl.BlockSpec((tq, q.shape[1]), cur),
            pl.BlockSpec((WINDOW, kvw), prev),
            pl.BlockSpec((WINDOW, kvw), cur),
            pl.BlockSpec((WINDOW, kvw), prev),
            pl.BlockSpec((WINDOW, kvw), cur),
            pl.BlockSpec((KV_B, G_B * tq, 1), lambda n: (0, 0, 0)),
        ],
        out_specs=pl.BlockSpec((tq, q.shape[1]), cur),
        out_shape=jax.ShapeDtypeStruct(q.shape, jnp.bfloat16),
        compiler_params=_params("parallel"),
        name="window_attn",
    )(q, k_prev, k_cur, v_prev, v_cur, sink_rows)


def _trunk(x, c_mod, kv_mod, f_mod, pos, weights, *, tm, sparse_attn, make_ctx):
    (g_norm, w_in_a, w_out_a, g_kv, w_kv_b, w_q_b, sinks_b, w_out_b, w_gate_d, w_up_d, w_down_d,
     w_router, b_router, w_gate_e, w_up_e, w_down_e, g_final) = weights
    b, t, d = x.shape
    m = b * t
    per_row = tm > t
    tps = 1 if per_row else t // tm

    def seq(v):
        if per_row:
            return jnp.repeat(v, t, axis=0)[None]
        return v[:, None, :]

    pos_rows = jnp.tile(pos, b) if per_row else pos
    tab_a = _rope_table(pos_rows, HD_A, ROT_A)
    tab_i = _rope_table(pos_rows, HD_IDX, ROT_IDX)
    tab_t = _tail_table(pos_rows)
    tab_b = _rope_table(pos_rows, HD_B, ROT_B)

    xf = x.reshape(m, d)
    rows_a = []
    ctx = None
    for l in range(DEPTH):
        if l == N_A_LAYERS:
            shift, scale = jnp.split(kv_mod, 2, axis=-1)
            h = _adaln(xf, g_kv, seq(shift), seq(scale), tm=tm, tiles_per_seq=tps)
            kvw_b = KV_B * HD_B
            kb, kb_bf = _matmul(h, w_kv_b, col0=0, ncols=kvw_b, tm=tm, tn=kvw_b, rope=(tab_b, ROT_B // 2),
                                out_dtypes=(jnp.float32, jnp.bfloat16))
            vb, vb_bf = _matmul(h, w_kv_b, col0=kvw_b, ncols=kvw_b, tm=tm, tn=kvw_b,
                                out_dtypes=(jnp.float32, jnp.bfloat16))
            ctx = make_ctx(kb.reshape(b, t, KV_B, HD_B), vb.reshape(b, t, KV_B, HD_B), kb_bf, vb_bf)
        sh1, sc1, g1, sh2, sc2, g2 = jnp.split(c_mod[l], 6, axis=-1)
        h = _adaln(xf, g_norm[l, 0], seq(sh1), seq(sc1), tm=tm, tiles_per_seq=tps)
        if l < N_A_LAYERS:
            both = (jnp.float32, jnp.bfloat16)
            only_bf = (jnp.bfloat16,)
            kvw = KV_A * HD_A
            q = _matmul(h, w_in_a, layer=l, col0=0, ncols=QW_A, tm=tm, tn=512, rope=(tab_a, ROT_A // 2),
                        out_dtypes=only_bf)
            k, k_bf = _matmul(h, w_in_a, layer=l, col0=QW_A, ncols=kvw, tm=tm, tn=kvw,
                              rope=(tab_a, ROT_A // 2), out_dtypes=both)
            v, v_bf = _matmul(h, w_in_a, layer=l, col0=QW_A + kvw, ncols=kvw, tm=tm, tn=kvw, out_dtypes=both)
            qi = _matmul(h, w_in_a, layer=l, col0=QW_A + 2 * kvw, ncols=H_IDX * HD_IDX, tm=tm, tn=512,
                         rope=(tab_i, ROT_IDX // 2), out_dtypes=only_bf)
            w_tail = jnp.pad(w_in_a[l][:, QW_A + 2 * kvw + H_IDX * HD_IDX:],
                             ((0, 0), (0, LANES - HD_IDX - H_IDX)))
            tail = _matmul(h, w_tail, tm=tm, tn=LANES, rope=(tab_t, ROT_IDX // 2))
            o = sparse_attn(l, q, qi, k_bf, v_bf, tail)
            rows_a.append((k.reshape(b, t, KV_A, HD_A), v.reshape(b, t, KV_A, HD_A),
                           tail[:, :HD_IDX].reshape(b, t, HD_IDX)))
            w_o, lo = w_out_a, l
        else:
            lb = l - N_A_LAYERS
            q = _matmul(h, w_q_b, layer=lb, tm=tm, tn=512, rope=(tab_b, ROT_B // 2),
                        out_dtypes=(jnp.bfloat16,))
            o = ctx[0](q, sinks_b[lb])
            w_o, lo = w_out_b, lb
        xf = _matmul(o, w_o, layer=lo, tm=tm, tn=512, res=(xf, seq(g1), tps))
        if l % 2 == 0:
            h = _adaln(xf, g_norm[l, 1], seq(sh2), seq(sc2), tm=tm, tiles_per_seq=tps)
            xf = _ffn(h, w_gate_d, w_up_d, w_down_d, layer=l // 2, tm=min(tm, 512), tf=256,
                      res=(xf, seq(g2), 1 if per_row else t // min(tm, 512)))
        else:
            h, logits = _adaln(xf, g_norm[l, 1], seq(sh2), seq(sc2), tm=tm, tiles_per_seq=tps,
                               out_dtype=jnp.float32, router=(w_router[l // 2], b_router[l // 2]))
            tmc = min(tm, 256)
            xf = _moe(h, logits, w_gate_e, w_up_e, w_down_e, layer=l // 2, tm=tm, tf=256,
                      res=(xf, seq(g2), tmc, 1 if per_row else t // tmc))
    fsh, fsc = jnp.split(f_mod, 2, axis=-1)
    y = _adaln(xf, g_final, seq(fsh), seq(fsc), tm=tm, tiles_per_seq=tps, out_dtype=jnp.float32)
    return y.reshape(b, t, d), rows_a, ctx


def kernel(x_prompt, x_sample, cache_k_a, cache_v_a, cache_kidx_a, state_k_b, state_v_b, page_table,
           c_prompt, c_sample, g_norm, w_mod, b_mod, w_in_a, w_out_a, g_kv, w_kvmod, b_kvmod, w_kv_b,
           w_q_b, sinks_b, w_out_b, w_gate_d, w_up_d, w_down_d, w_router, b_router, w_gate_e, w_up_e,
           w_down_e, g_final, w_fmod, b_fmod):
    n_pages = page_table.shape[1]
    past = n_pages * PAGE_SIZE
    nb_p, nb_s = c_prompt.shape[0], c_sample.shape[0]

    c_rows = 48
    c_all = jnp.zeros((c_rows, D_MODEL), jnp.bfloat16)
    c_all = c_all.at[:nb_p + nb_s].set(jnp.concatenate([c_prompt, c_sample], axis=0).astype(jnp.bfloat16))
    mods = [_matmul(c_all, w_mod, layer=l, tm=c_rows, tn=512, bias=b_mod[l][None]) for l in range(DEPTH)]
    kvm = _matmul(c_all, w_kvmod, tm=c_rows, tn=512, bias=b_kvmod[None])
    fm = _matmul(c_all, w_fmod, tm=c_rows, tn=512, bias=b_fmod[None])

    sb, st = x_sample.shape[:2]
    w_buf = state_k_b.shape[1]
    kvw_b = KV_B * HD_B

    def prompt_sparse_attn(li, q, qi, k_bf, v_bf, tail):
        b, t = x_prompt.shape[:2]
        return _prompt_sparse_attn_t(q, qi, k_bf, v_bf, tail, b=b, t=t, tq=256, tk=256,
                                   n_sel=min(TOPK_MAX, t // 4))

    def sample_sparse_attn(li, q, qi, k_bf, v_bf, tail):
        r3 = lambda a: a.reshape(sb, st, a.shape[-1])
        return _sample_sparse_attn(
            r3(q), r3(qi), r3(tail[:, HD_IDX:HD_IDX + H_IDX]), r3(tail[:, :HD_IDX]), r3(k_bf), r3(v_bf),
            cache_k_a, cache_v_a, cache_kidx_a, page_table, layer=li, n_sel=min(TOPK_MAX, (past + st) // 4))

    def prompt_ctx(kb, vb, kb_bf, vb_bf):
        t = kb.shape[1]
        wp = min(WINDOW, t)

        vb_t = vb_bf.T

        def attend(q, sinks):
            return _window_attn_t(q, kb_bf, vb_t, sinks, blocks_per_seq=t // WINDOW)

        return attend, kb[:, t - wp:], vb[:, t - wp:]

    def sample_ctx(kb, vb, kb_bf, vb_bf):
        tq = 16
        pad_new = lambda a: jnp.pad(a.reshape(sb, st, kvw_b), ((0, 0), (0, WINDOW - st), (0, 0))).reshape(-1, kvw_b)
        k_prev = state_k_b.astype(jnp.bfloat16).reshape(sb * w_buf, kvw_b)
        v_prev = state_v_b.astype(jnp.bfloat16).reshape(sb * w_buf, kvw_b)
        k_cur, v_cur = pad_new(kb_bf), pad_new(vb_bf)

        def attend(q, sinks):
            qp = jnp.pad(q.reshape(sb, st, -1), ((0, 0), (0, tq - st), (0, 0))).reshape(sb * tq, -1)
            o = _window_attn(qp, k_prev, k_cur, v_prev, v_cur, sinks, tq=tq, blocks_per_seq=1, mask_first=False)
            return o.reshape(sb, tq, -1)[:, :st].reshape(sb * st, -1)

        k_ctx = jnp.concatenate([state_k_b, kb], axis=1)
        v_ctx = jnp.concatenate([state_v_b, vb], axis=1)
        return attend, k_ctx[:, -w_buf:], v_ctx[:, -w_buf:]

    weights = (g_norm, w_in_a, w_out_a, g_kv, w_kv_b, w_q_b, sinks_b, w_out_b, w_gate_d, w_up_d, w_down_d,
               w_router, b_router, w_gate_e, w_up_e, w_down_e, g_final)
    pos_p = jnp.arange(x_prompt.shape[1])
    pos_s = past + jnp.arange(x_sample.shape[1])
    y_prompt, rows_p, ctx_p = _trunk(
        x_prompt, [mm[:nb_p] for mm in mods], kvm[:nb_p], fm[:nb_p], pos_p, weights,
        tm=1024, sparse_attn=prompt_sparse_attn, make_ctx=prompt_ctx)
    y_sample, rows_s, ctx_s = _trunk(
        x_sample, [mm[nb_p:nb_p + nb_s] for mm in mods], kvm[nb_p:nb_p + nb_s], fm[nb_p:nb_p + nb_s], pos_s,
        weights, tm=x_sample.shape[0] * x_sample.shape[1], sparse_attn=sample_sparse_attn, make_ctx=sample_ctx)

    return (y_prompt, y_sample,
            jnp.stack([r[0] for r in rows_p]), jnp.stack([r[1] for r in rows_p]), jnp.stack([r[2] for r in rows_p]),
            jnp.stack([r[0] for r in rows_s]), jnp.stack([r[1] for r in rows_s]), jnp.stack([r[2] for r in rows_s]),
            ctx_p[1], ctx_p[2], ctx_s[1], ctx_s[2])
```

```python
import functools
import math

import jax
import jax.numpy as jnp
import numpy as np
from jax import lax
from jax.experimental import pallas as pl
from jax.experimental.pallas import tpu as pltpu

D_MODEL = 2048
DEPTH = 4
PAGE_SIZE = 128
HD_A = 128
H_A = D_MODEL // HD_A
KV_A = 4
G_A = H_A // KV_A
QW_A = H_A * HD_A
H_IDX = 16
HD_IDX = 64
TOPK_MAX = 256
IDX_W_SCALE = (H_IDX * HD_IDX) ** -0.5
Q_BLOCK = 128
HD_B = 64
H_B = D_MODEL // HD_B
KV_B = 4
G_B = H_B // KV_B
QW_B = H_B * HD_B
KVW_B = 2 * KV_B * HD_B
WINDOW = 128
ROPE_THETA = 500000.0
ROT_A = HD_A // 4
ROT_IDX = HD_IDX // 4
ROT_B = HD_B // 4
N_EXPERTS = 8
TOP_K_EXPERTS = 2
EPS = 1e-6
N_A_LAYERS = DEPTH // 2

LANES = 128
VMEM_LIMIT = 56 * 1024 * 1024
MOE_ROW_TILE = 1152


def _params(*sem):
    return pltpu.CompilerParams(dimension_semantics=sem, vmem_limit_bytes=VMEM_LIMIT)


def _rope_table(pos, head_dim, rot):
    half = rot // 2
    inv = jnp.exp(-math.log(ROPE_THETA) * jnp.arange(half, dtype=jnp.float32) / half)
    ang = pos.astype(jnp.float32)[:, None] * inv[None, :]
    cos, sin = jnp.cos(ang), jnp.sin(ang)
    d = np.arange(LANES) % head_dim
    idx = np.where(d < half, d, np.clip(d - half, 0, half - 1))
    cos_l, sin_l = cos[:, idx], sin[:, idx]
    c = jnp.where((d < rot)[None], cos_l, 1.0)
    u = jnp.where(((d >= half) & (d < rot))[None], sin_l, 0.0)
    dn = jnp.where((d < half)[None], -sin_l, 0.0)
    return jnp.concatenate([c, u, dn], axis=1)


def _tail_table(pos):
    t = _rope_table(pos, HD_IDX, ROT_IDX)
    lane = np.arange(LANES)
    c = jnp.where((lane < HD_IDX)[None], t[:, :LANES],
                  jnp.where((lane < HD_IDX + H_IDX)[None], IDX_W_SCALE, 1.0))
    keep = (lane < HD_IDX)[None]
    u = jnp.where(keep, t[:, LANES:2 * LANES], 0.0)
    dn = jnp.where(keep, t[:, 2 * LANES:], 0.0)
    return jnp.concatenate([c, u, dn], axis=1)


def _adaln_kernel(*refs, has_router):
    x_ref, g_ref, sh_ref, sc_ref = refs[:4]
    x = x_ref[...]
    y = x * lax.rsqrt(jnp.mean(x * x, axis=-1, keepdims=True) + EPS)
    h = (y * g_ref[...]) * (1.0 + sc_ref[0]) + sh_ref[0]
    if has_router:
        wr_ref, br_ref, h_ref, lg_ref = refs[4:]
        h_ref[...] = h.astype(h_ref.dtype)
        lg_ref[...] = jnp.dot(h.astype(jnp.bfloat16), wr_ref[...], preferred_element_type=jnp.float32) + br_ref[...]
    else:
        h_ref = refs[4]
        h_ref[...] = h.astype(h_ref.dtype)


def _adaln(x, g, shift, scale, *, tm, tiles_per_seq, out_dtype=jnp.bfloat16, router=None):
    m, d = x.shape
    r = shift.shape[1]
    in_specs = [
        pl.BlockSpec((tm, d), lambda i: (i, 0)),
        pl.BlockSpec((1, d), lambda i: (0, 0)),
        pl.BlockSpec((1, r, d), lambda i: (i // tiles_per_seq, 0, 0)),
        pl.BlockSpec((1, r, d), lambda i: (i // tiles_per_seq, 0, 0)),
    ]
    args = [x, g.reshape(1, d), shift, scale]
    out_shape = [jax.ShapeDtypeStruct((m, d), out_dtype)]
    out_specs = [pl.BlockSpec((tm, d), lambda i: (i, 0))]
    if router is not None:
        w_r, b_r = router
        wr = jnp.zeros((d, LANES), jnp.bfloat16).at[:, :N_EXPERTS].set(w_r.astype(jnp.bfloat16))
        br = jnp.zeros((1, LANES), jnp.float32).at[0, :N_EXPERTS].set(b_r.astype(jnp.float32))
        in_specs += [pl.BlockSpec((d, LANES), lambda i: (0, 0)), pl.BlockSpec((1, LANES), lambda i: (0, 0))]
        args += [wr, br]
        out_shape.append(jax.ShapeDtypeStruct((m, LANES), jnp.float32))
        out_specs.append(pl.BlockSpec((tm, LANES), lambda i: (i, 0)))
    outs = pl.pallas_call(
        functools.partial(_adaln_kernel, has_router=router is not None),
        grid=(m // tm,),
        in_specs=in_specs,
        out_specs=out_specs,
        out_shape=out_shape,
        compiler_params=_params("parallel"),
        name="adaln",
    )(*args)
    return outs if router is not None else outs[0]


def _mm_kernel(*refs, tn, has_bias, rope_half, has_res):
    it = iter(refs)
    a_ref = next(it)
    w_ref = next(it)
    bias_ref = next(it) if has_bias else None
    tab_ref = next(it) if rope_half else None
    res_ref = next(it) if has_res else None
    gate_ref = next(it) if has_res else None
    out_refs = list(it)
    acc = jnp.dot(a_ref[...], w_ref[...].astype(jnp.bfloat16), preferred_element_type=jnp.float32)
    if has_bias:
        acc = acc + bias_ref[...]
    if rope_half:
        c = tab_ref[:, :LANES]
        u = tab_ref[:, LANES:2 * LANES]
        dn = tab_ref[:, 2 * LANES:]
        parts = []
        for grp in range(tn // LANES):
            xg = acc[:, grp * LANES:(grp + 1) * LANES]
            parts.append(xg * c + pltpu.roll(xg, rope_half, 1) * u + pltpu.roll(xg, LANES - rope_half, 1) * dn)
        acc = parts[0] if len(parts) == 1 else jnp.concatenate(parts, axis=1)
    if has_res:
        acc = res_ref[...] + gate_ref[0] * acc
    for o in out_refs:
        o[...] = acc.astype(o.dtype)


def _matmul(a, w, *, layer=0, col0=0, ncols=None, tm, tn, bias=None, rope=None, res=None,
            out_dtypes=(jnp.float32,)):
    if w.ndim == 2:
        w = w[None]
    m, k = a.shape
    ncols = w.shape[2] - col0 if ncols is None else ncols
    assert ncols % tn == 0 and col0 % tn == 0 and m % tm == 0
    cb = col0 // tn
    in_specs = [
        pl.BlockSpec((tm, k), lambda i, j: (i, 0)),
        pl.BlockSpec((None, k, tn), lambda i, j: (layer, 0, cb + j)),
    ]
    args = [a, w]
    if bias is not None:
        in_specs.append(pl.BlockSpec((1, tn), lambda i, j: (0, cb + j)))
        args.append(bias)
    rope_half = 0
    if rope is not None:
        tab, rope_half = rope
        tab_tiles = tab.shape[0] // tm
        in_specs.append(pl.BlockSpec((tm, 3 * LANES), lambda i, j: (i % tab_tiles, 0)))
        args.append(tab)
    if res is not None:
        resid, gate, tps = res
        r = gate.shape[1]
        in_specs.append(pl.BlockSpec((tm, tn), lambda i, j: (i, j)))
        in_specs.append(pl.BlockSpec((1, r, tn), lambda i, j: (i // tps, 0, j)))
        args += [resid, gate]
    outs = pl.pallas_call(
        functools.partial(_mm_kernel, tn=tn, has_bias=bias is not None, rope_half=rope_half,
                          has_res=res is not None),
        grid=(m // tm, ncols // tn),
        in_specs=in_specs,
        out_specs=[pl.BlockSpec((tm, tn), lambda i, j: (i, j)) for _ in out_dtypes],
        out_shape=[jax.ShapeDtypeStruct((m, ncols), dt) for dt in out_dtypes],
        compiler_params=_params("parallel", "arbitrary"),
        name="matmul",
    )(*args)
    return outs if len(outs) > 1 else outs[0]


def _start_row_gather(idx_ref, base, src_hbm, dst, sem, n_rows):
    def issue(r, carry):
        pltpu.make_async_copy(src_hbm.at[pl.ds(idx_ref[base + r], 1)], dst.at[pl.ds(r, 1)], sem).start()
        return carry

    lax.fori_loop(0, n_rows, issue, 0, unroll=8)


def _wait_row_gather(src_hbm, dst, sem, n_rows):
    pltpu.make_async_copy(src_hbm.at[pl.ds(0, n_rows)], dst, sem).wait()


def _ffn_kernel(te_ref, tv_ref, rt_ref, *refs, nj, mode, tm):
    del te_ref
    if mode == "dense":
        x_ref, wg_ref, wu_ref, wd_ref, res_ref, gate_ref, o_ref = refs
    else:
        x_hbm, wg_ref, wu_ref, wd_ref, rs_ref, o_ref, xf_scr, x_ref, sem = refs
    i = pl.program_id(0)
    j = pl.program_id(1)

    @pl.when(j == 0)
    def _():
        o_ref[...] = jnp.zeros_like(o_ref)

    if mode == "routed":
        @pl.when((j == 0) & (tv_ref[i] > 0))
        def _():
            @pl.when(i == 0)
            def _():
                _start_row_gather(rt_ref, 0, x_hbm, xf_scr, sem, tm)

            _wait_row_gather(x_hbm, xf_scr, sem, tm)
            x_ref[...] = xf_scr[...].astype(x_ref.dtype)

            last = pl.num_programs(0) - 1

            @pl.when((i < last) & (tv_ref[jnp.minimum(i + 1, last)] > 0))
            def _():
                _start_row_gather(rt_ref, (i + 1) * tm, x_hbm, xf_scr, sem, tm)

    @pl.when(tv_ref[i] > 0)
    def _():
        x = x_ref[...]
        g = jnp.dot(x, wg_ref[...].astype(jnp.bfloat16), preferred_element_type=jnp.float32)
        u = jnp.dot(x, wu_ref[...].astype(jnp.bfloat16), preferred_element_type=jnp.float32)
        act = (g * (1.0 / (1.0 + jnp.exp(-g)))) * u
        o_ref[...] += jnp.dot(act.astype(jnp.bfloat16), wd_ref[...].astype(jnp.bfloat16),
                              preferred_element_type=jnp.float32)

    @pl.when(j == nj - 1)
    def _():
        if mode == "dense":
            o_ref[...] = res_ref[...] + gate_ref[0] * o_ref[...]
        else:
            o_ref[...] = rs_ref[...] * o_ref[...]


def _ffn(x, wg, wu, wd, *, layer, tm, tf, tile_expert=None, tile_valid=None, row_token=None, res=None,
         row_scale=None):
    d = x.shape[1]
    f = wg.shape[-1]
    dense = res is not None
    m = x.shape[0] if dense else row_token.shape[0]
    nt, nj = m // tm, f // tf
    if dense:
        tile_expert = jnp.full((nt,), layer, jnp.int32)
        tile_valid = jnp.ones((nt,), jnp.int32)
        row_token = jnp.zeros((1,), jnp.int32)
    else:
        tile_expert = tile_expert + layer * wg.shape[1]
        wg, wu, wd = wg.reshape(-1, d, f), wu.reshape(-1, d, f), wd.reshape(-1, f, d)

    def jj(i, j, tv):
        return jnp.where(tv[i] > 0, j, nj - 1)

    in_specs = [
        pl.BlockSpec((tm, d), lambda i, j, te, tv, rt: (i, 0)) if dense else pl.BlockSpec(memory_space=pl.ANY),
        pl.BlockSpec((None, d, tf), lambda i, j, te, tv, rt: (te[i], 0, jj(i, j, tv))),
        pl.BlockSpec((None, d, tf), lambda i, j, te, tv, rt: (te[i], 0, jj(i, j, tv))),
        pl.BlockSpec((None, tf, d), lambda i, j, te, tv, rt: (te[i], jj(i, j, tv), 0)),
    ]
    args = [x, wg, wu, wd]
    scratch = []
    if dense:
        resid, gate, tps = res
        r = gate.shape[1]
        in_specs.append(pl.BlockSpec((tm, d), lambda i, j, te, tv, rt: (i, 0), pipeline_mode=pl.Buffered(1)))
        in_specs.append(pl.BlockSpec((1, r, d), lambda i, j, te, tv, rt: (i // tps, 0, 0)))
        args += [resid, gate]
    else:
        in_specs.append(pl.BlockSpec((tm, 1), lambda i, j, te, tv, rt: (i, 0)))
        args.append(row_scale)
        scratch = [pltpu.VMEM((tm, d), jnp.float32), pltpu.VMEM((tm, d), jnp.bfloat16),
                   pltpu.SemaphoreType.DMA(())]
    return pl.pallas_call(
        functools.partial(_ffn_kernel, nj=nj, mode="dense" if dense else "routed", tm=tm),
        grid_spec=pltpu.PrefetchScalarGridSpec(
            num_scalar_prefetch=3,
            grid=(nt, nj),
            in_specs=in_specs,
            out_specs=pl.BlockSpec((tm, d), lambda i, j, te, tv, rt: (i, 0)),
            scratch_shapes=scratch,
        ),
        out_shape=jax.ShapeDtypeStruct((m, d), jnp.float32),
        compiler_params=_params("parallel" if dense else "arbitrary", "arbitrary"),
        name="ffn_dense" if dense else "ffn_routed",
    )(tile_expert, tile_valid, row_token, *args)


def _combine_kernel(dest_ref, ys_hbm, res_ref, gate_ref, o_ref, buf, sem, *, tm):
    i = pl.program_id(0)
    npair_half = pl.num_programs(0) * tm

    def issue(r, carry):
        for k in range(TOP_K_EXPERTS):
            row = dest_ref[k * npair_half + i * tm + r]
            pltpu.make_async_copy(ys_hbm.at[pl.ds(row, 1)], buf.at[pl.ds(k * tm + r, 1)], sem).start()
        return carry

    lax.fori_loop(0, tm, issue, 0, unroll=4)
    pltpu.make_async_copy(ys_hbm.at[pl.ds(0, TOP_K_EXPERTS * tm)], buf, sem).wait()
    f = buf[0:tm]
    for k in range(1, TOP_K_EXPERTS):
        f = f + buf[k * tm:(k + 1) * tm]
    o_ref[...] = res_ref[...] + gate_ref[0] * f


def _moe_combine(ys, dest, res, gate, *, tm, tiles_per_seq):
    n, d = res.shape
    r = gate.shape[1]
    dest_km = dest.reshape(n, TOP_K_EXPERTS).T.reshape(-1)
    return pl.pallas_call(
        functools.partial(_combine_kernel, tm=tm),
        grid_spec=pltpu.PrefetchScalarGridSpec(
            num_scalar_prefetch=1,
            grid=(n // tm,),
            in_specs=[
                pl.BlockSpec(memory_space=pl.ANY),
                pl.BlockSpec((tm, d), lambda i, dr: (i, 0)),
                pl.BlockSpec((1, r, d), lambda i, dr: (i // tiles_per_seq, 0, 0)),
            ],
            out_specs=pl.BlockSpec((tm, d), lambda i, dr: (i, 0)),
            scratch_shapes=[pltpu.VMEM((TOP_K_EXPERTS * tm, d), jnp.float32), pltpu.SemaphoreType.DMA(())],
        ),
        out_shape=jax.ShapeDtypeStruct((n, d), jnp.float32),
        compiler_params=_params("arbitrary"),
        name="moe_combine",
    )(dest_km, ys, res, gate)


def _moe(hs, logits_list, wg, wu, wd, *, layer, tm, tf, res_list):
    h = jnp.concatenate(hs, axis=0) if len(hs) > 1 else hs[0]
    logits = jnp.concatenate(logits_list, axis=0) if len(hs) > 1 else logits_list[0]
    n, d = h.shape
    vals, idx = lax.top_k(logits[:, :N_EXPERTS], TOP_K_EXPERTS)
    gates = jax.nn.softmax(vals, axis=-1)
    npair = n * TOP_K_EXPERTS
    nt = npair // tm + N_EXPERTS
    flat_e = idx.reshape(-1).astype(jnp.int32)
    order = jnp.argsort(flat_e, stable=True)
    sorted_e = flat_e[order]
    counts = jnp.sum(flat_e[:, None] == jnp.arange(N_EXPERTS)[None, :], axis=0).astype(jnp.int32)
    tiles_e = (counts + tm - 1) // tm
    tile_end = jnp.cumsum(tiles_e)
    tile_start = tile_end - tiles_e
    grp_start = jnp.cumsum(counts) - counts
    rank = jnp.arange(npair, dtype=jnp.int32) - grp_start[sorted_e]
    dest_sorted = tile_start[sorted_e] * tm + rank
    dest = jnp.zeros((npair,), jnp.int32).at[order].set(dest_sorted)
    rows = nt * tm
    row_token = jnp.zeros((rows,), jnp.int32).at[dest].set(jnp.arange(npair, dtype=jnp.int32) // TOP_K_EXPERTS)
    row_gate = jnp.zeros((rows,), jnp.float32).at[dest].set(gates.reshape(-1))
    tile_ids = jnp.arange(nt, dtype=jnp.int32)
    total = tile_end[-1]
    tile_valid = (tile_ids < total).astype(jnp.int32)
    last_e = jnp.searchsorted(tile_end, total - 1, side="right").astype(jnp.int32)
    tile_expert = jnp.where(tile_valid > 0, jnp.searchsorted(tile_end, tile_ids, side="right").astype(jnp.int32),
                            last_e)
    tile_expert = jnp.clip(tile_expert, 0, N_EXPERTS - 1)
    ys = _ffn(h, wg, wu, wd, layer=layer, tm=tm, tf=tf, tile_expert=tile_expert, tile_valid=tile_valid,
              row_token=row_token, row_scale=row_gate[:, None])
    outs = []
    start = 0
    for hh, (resid, gate, tmc, tps) in zip(hs, res_list):
        stop = start + hh.shape[0] * TOP_K_EXPERTS
        outs.append(_moe_combine(ys, dest[start:stop], resid, gate, tm=tmc, tiles_per_seq=tps))
        start = stop
    return outs


INT_MIN = -(2 ** 31)
NEG = -1e30


def _sortable(score):
    bits = pltpu.bitcast(score, jnp.int32)
    return bits ^ ((bits >> 31) & jnp.int32(0x7FFFFFFF))


def _kth_largest(count_ge, shape, n_sel):
    def step(s, t):
        cand = t + lax.shift_left(jnp.int32(1), 31 - s)
        return jnp.where(count_ge(cand) >= n_sel, cand, t)

    return lax.fori_loop(0, 32, step, jnp.full(shape, INT_MIN, jnp.int32))


def _psat_kernel(q_ref, qi_ref, k_ref, vt_ref, tail_ref, wt_ref, o_ref, key_scr, m_scr, l_scr, acc_scr, *,
                 tq, tk, n_sel):
    i = pl.program_id(1)
    nck = (i + 1) * (tq // tk)
    row0 = i * tq
    nt = (((1,), (1,)), ((), ()))
    lane = lax.broadcasted_iota(jnp.int32, (tk, LANES), 1)
    kidx0 = lax.broadcasted_iota(jnp.int32, (tk, tq), 0)
    qidx = row0 + lax.broadcasted_iota(jnp.int32, (tk, tq), 1)

    def score_chunk(c, carry):
        ks = pl.multiple_of(c * tk, tk)
        kf = jnp.where(lane < HD_IDX, tail_ref[pl.ds(ks, tk), :], 0.0)
        k_even = kf.astype(jnp.bfloat16)
        k_odd = pltpu.roll(kf, HD_IDX, 1).astype(jnp.bfloat16)
        sc = jnp.zeros((tk, tq), jnp.float32)
        for h in range(H_IDX):
            qp = qi_ref[:, (h // 2) * LANES:(h // 2 + 1) * LANES]
            d = lax.dot_general(k_even if h % 2 == 0 else k_odd, qp, nt, preferred_element_type=jnp.float32)
            sc = sc + jnp.maximum(d, 0.0) * wt_ref[h:h + 1, :]
        key_scr[c] = jnp.where(ks + kidx0 <= qidx, _sortable(sc), INT_MIN)
        return carry

    lax.fori_loop(0, nck, score_chunk, 0)

    def count_ge(cand):
        def chunk(c, acc):
            hit = jnp.where(key_scr[c] >= cand, 1.0, 0.0)
            return acc + jnp.sum(hit.reshape(tk // 8, 8, tq), axis=0)

        acc = lax.fori_loop(0, nck, chunk, jnp.zeros((8, tq), jnp.float32))
        return jnp.sum(acc, axis=0, keepdims=True)

    thr = jnp.maximum(_kth_largest(count_ge, (1, tq), float(n_sel)), INT_MIN + 1)

    need = float(n_sel) - count_ge(thr + 1)
    tri = (lax.broadcasted_iota(jnp.int32, (tk, tk), 0) >= lax.broadcasted_iota(jnp.int32, (tk, tk), 1))
    tri = jnp.where(tri, 1.0, 0.0).astype(jnp.bfloat16)

    def select_chunk(c, ties_before):
        kk = key_scr[c]
        eq = kk == thr
        rank = jnp.dot(tri, jnp.where(eq, 1.0, 0.0).astype(jnp.bfloat16),
                       preferred_element_type=jnp.float32) + ties_before
        key_scr[c] = jnp.where((kk > thr) | (eq & (rank <= need)), 1, 0)
        return rank[tk - 1:tk, :]

    lax.fori_loop(0, nck, select_chunk, jnp.zeros((1, tq), jnp.float32))

    m_scr[...] = jnp.full_like(m_scr, NEG)
    l_scr[...] = jnp.zeros_like(l_scr)
    acc_scr[...] = jnp.zeros_like(acc_scr)
    c_exp = (HD_A ** -0.5) * math.log2(math.e)

    def att_chunk(c, carry):
        ks = pl.multiple_of(c * tk, tk)
        bias = jnp.where(key_scr[c] > 0, 0.0, NEG)
        bias = jnp.concatenate([bias] * G_A, axis=1)
        for g in range(KV_A):
            qg = jnp.concatenate(
                [q_ref[:, (g * G_A + j) * HD_A:(g * G_A + j + 1) * HD_A] for j in range(G_A)], axis=0)
            kg = k_ref[pl.ds(ks, tk), g * HD_A:(g + 1) * HD_A]
            s = lax.dot_general(kg, qg, nt, preferred_element_type=jnp.float32) + bias
            m_old = m_scr[g]
            m_new = jnp.maximum(m_old, jnp.max(s, axis=0, keepdims=True))
            alpha = jnp.exp2((m_old - m_new) * c_exp)
            p = jnp.exp2((s - m_new) * c_exp)
            l_scr[g] = alpha * l_scr[g] + jnp.sum(p, axis=0, keepdims=True)
            vtg = vt_ref[c, g * HD_A:(g + 1) * HD_A, :]
            acc_scr[g] = alpha * acc_scr[g] + jnp.dot(vtg, p.astype(jnp.bfloat16),
                                                      preferred_element_type=jnp.float32)
            m_scr[g] = m_new
        return carry

    lax.fori_loop(0, nck, att_chunk, 0)

    for g in range(KV_A):
        og = acc_scr[g] * (1.0 / l_scr[g])
        for j in range(G_A):
            h = g * G_A + j
            o_ref[:, h * HD_A:(h + 1) * HD_A] = og[:, j * tq:(j + 1) * tq].T.astype(o_ref.dtype)


def _prompt_sparse_attn_t(q, qi, k, v, tail, *, b, t, tq, tk, n_sel):
    nqt = t // tq
    nck = t // tk
    kvw = v.shape[1]
    vt = v.reshape(b, nck, tk, kvw).transpose(0, 1, 3, 2)
    wt = tail[:, HD_IDX:HD_IDX + H_IDX].T
    return pl.pallas_call(
        functools.partial(_psat_kernel, tq=tq, tk=tk, n_sel=n_sel),
        grid=(b, nqt),
        in_specs=[
            pl.BlockSpec((tq, q.shape[1]), lambda bb, i: (bb * nqt + i, 0)),
            pl.BlockSpec((tq, qi.shape[1]), lambda bb, i: (bb * nqt + i, 0)),
            pl.BlockSpec((t, k.shape[1]), lambda bb, i: (bb, 0)),
            pl.BlockSpec((None, nck, kvw, tk), lambda bb, i: (bb, 0, 0, 0)),
            pl.BlockSpec((t, LANES), lambda bb, i: (bb, 0)),
            pl.BlockSpec((H_IDX, tq), lambda bb, i: (0, bb * nqt + i)),
        ],
        out_specs=pl.BlockSpec((tq, q.shape[1]), lambda bb, i: (bb * nqt + i, 0)),
        out_shape=jax.ShapeDtypeStruct(q.shape, jnp.bfloat16),
        scratch_shapes=[
            pltpu.VMEM((nck, tk, tq), jnp.int32),
            pltpu.VMEM((KV_A, 1, G_A * tq), jnp.float32),
            pltpu.VMEM((KV_A, 1, G_A * tq), jnp.float32),
            pltpu.VMEM((KV_A, HD_A, G_A * tq), jnp.float32),
        ],
        compiler_params=_params("parallel", "arbitrary"),
        name="prompt_sparse_attn_t",
    )(q, qi, k, vt, tail, wt)


QPAD = 8
PAGES_PER_STEP = 16
PAGES_PER_INDEX_STEP = 32


def _ssi_kernel(pt_ref, qi_ref, w_ref, kn_ref, *refs, pp, n_sel, s_new, ns):
    del pt_ref
    page_refs = refs[:pp]
    bias_ref, bias_new_ref, key_scr = refs[pp:]
    s = pl.program_id(1)
    n = pp * PAGE_SIZE

    def scores(kt):
        d = jnp.dot(qi_ref[...], kt, preferred_element_type=jnp.float32)
        sc = jnp.maximum(d, 0.0) * w_ref[...]
        return jnp.sum(sc.reshape(QPAD, H_IDX, kt.shape[1]), axis=1)

    kt = jnp.concatenate([r[...] for r in page_refs], axis=1).astype(jnp.bfloat16)
    qrow = lax.broadcasted_iota(jnp.int32, (QPAD, n), 0)
    key_scr[s] = jnp.where(qrow < s_new, _sortable(scores(kt)), INT_MIN)

    @pl.when(s == ns - 1)
    def _():
        qr = lax.broadcasted_iota(jnp.int32, (QPAD, PAGE_SIZE), 0)
        col = lax.broadcasted_iota(jnp.int32, (QPAD, PAGE_SIZE), 1)
        key_new = jnp.where((col <= qr) & (qr < s_new), _sortable(scores(kn_ref[...])), INT_MIN)

        def count_ge(cand):
            acc = jnp.where(key_new >= cand, 1.0, 0.0)
            for c in range(ns):
                hit = jnp.where(key_scr[c] >= cand, 1.0, 0.0)
                for u in range(n // LANES):
                    acc = acc + hit[:, u * LANES:(u + 1) * LANES]
            return jnp.sum(acc, axis=1, keepdims=True)

        thr = jnp.maximum(_kth_largest(count_ge, (QPAD, 1), float(n_sel)), INT_MIN + 1)

        need = float(n_sel) - count_ge(thr + 1)
        nblk = n // LANES
        r_i = lax.broadcasted_iota(jnp.int32, (nblk * QPAD, nblk * QPAD), 0)
        c_i = lax.broadcasted_iota(jnp.int32, (nblk * QPAD, nblk * QPAD), 1)
        earlier = jnp.where((r_i % QPAD == c_i % QPAD) & (c_i // QPAD < r_i // QPAD), 1.0, 0.0).astype(jnp.bfloat16)
        upto = lax.broadcasted_iota(jnp.int32, (LANES, LANES), 0) <= lax.broadcasted_iota(jnp.int32, (LANES, LANES), 1)
        upto = jnp.where(upto, 1.0, 0.0).astype(jnp.bfloat16)

        def select(kk, ties_before, blocks):
            rep = lambda a: jnp.concatenate([a] * blocks, axis=0) if blocks > 1 else a
            eq = kk == rep(thr)
            rank = jnp.dot(jnp.where(eq, 1.0, 0.0).astype(jnp.bfloat16), upto, preferred_element_type=jnp.float32)
            if blocks > 1:
                rank = rank + jnp.dot(earlier, rank.astype(jnp.bfloat16),
                                      preferred_element_type=jnp.float32)[:, LANES - 1:LANES]
            rank = rank + rep(ties_before)
            sel = (kk > rep(thr)) | (eq & (rank <= rep(need)))
            return jnp.where(sel, 0.0, NEG), rank[(blocks - 1) * QPAD:, LANES - 1:LANES]

        ties = jnp.zeros((QPAD, 1), jnp.float32)
        for c in range(ns):
            kc = key_scr[c]
            stacked = jnp.concatenate([kc[:, u * LANES:(u + 1) * LANES] for u in range(nblk)], axis=0)
            bias_c, ties = select(stacked, ties, nblk)
            for u in range(nblk):
                bias_ref[:, c * n + u * LANES:c * n + (u + 1) * LANES] = bias_c[u * QPAD:(u + 1) * QPAD]
        bias_new_ref[...] = select(key_new, ties, 1)[0]


def _ssa_kernel(pt_ref, q_ref, bias_ref, bias_new_ref, kn_ref, vn_ref, *refs, pp, ns):
    del pt_ref
    k_refs = refs[:pp]
    v_refs = refs[pp:2 * pp]
    o_ref, m_scr, l_scr, acc_scr = refs[2 * pp:]
    s = pl.program_id(1)
    nt = (((1,), (1,)), ((), ()))
    scale = HD_A ** -0.5

    @pl.when(s == 0)
    def _():
        m_scr[...] = jnp.full_like(m_scr, NEG)
        l_scr[...] = jnp.zeros_like(l_scr)
        acc_scr[...] = jnp.zeros_like(acc_scr)

    rows_g = G_A * QPAD

    def update(kg, vg, bias):
        bias = jnp.concatenate([bias] * (KV_A * G_A), axis=0)
        sc = jnp.concatenate(
            [lax.dot_general(q_ref[g], kg[g], nt, preferred_element_type=jnp.float32) for g in range(KV_A)],
            axis=0) * scale + bias
        m_old = m_scr[...]
        m_new = jnp.maximum(m_old, jnp.max(sc, axis=1, keepdims=True))
        alpha = jnp.exp(m_old - m_new)
        p = jnp.exp(sc - m_new)
        l_scr[...] = alpha * l_scr[...] + jnp.sum(p, axis=1, keepdims=True)
        pb = p.astype(jnp.bfloat16)
        pv = jnp.concatenate(
            [jnp.dot(pb[g * rows_g:(g + 1) * rows_g], vg[g], preferred_element_type=jnp.float32)
             for g in range(KV_A)], axis=0)
        acc_scr[...] = alpha * acc_scr[...] + pv
        m_scr[...] = m_new

    def head_rows(page_refs, g):
        return jnp.concatenate([r[pl.ds(g, PAGE_SIZE, stride=KV_A), :] for r in page_refs],
                               axis=0).astype(jnp.bfloat16)

    update([head_rows(k_refs, g) for g in range(KV_A)], [head_rows(v_refs, g) for g in range(KV_A)],
           bias_ref[...])

    @pl.when(s == ns - 1)
    def _():
        kn, vn = kn_ref[...], vn_ref[...]
        update([kn[:, g * HD_A:(g + 1) * HD_A] for g in range(KV_A)],
               [vn[:, g * HD_A:(g + 1) * HD_A] for g in range(KV_A)], bias_new_ref[...])
        o_ref[...] = acc_scr[...] / l_scr[...]


def _sample_sparse_attn(q, qi, wi, ki, k, v, cache_k, cache_v, cache_kidx, page_table, *, layer, n_sel):
    db, s_new = q.shape[:2]
    n_pages = page_table.shape[1]
    ppi, ppa = min(PAGES_PER_INDEX_STEP, n_pages), min(PAGES_PER_STEP, n_pages)
    nsi, nsa = n_pages // ppi, n_pages // ppa
    kvw = KV_A * HD_A
    rows = KV_A * G_A * QPAD
    pad_q = ((0, 0), (0, QPAD - s_new), (0, 0), (0, 0))
    pad_k = ((0, 0), (0, PAGE_SIZE - s_new), (0, 0))
    qi_r = jnp.pad(qi.reshape(db, s_new, H_IDX, HD_IDX), pad_q).reshape(db, QPAD * H_IDX, HD_IDX)
    w_r = jnp.pad(wi, ((0, 0), (0, QPAD - s_new), (0, 0))).reshape(db, QPAD * H_IDX, 1)
    ki_n = jnp.swapaxes(jnp.pad(ki.astype(jnp.bfloat16), pad_k), 1, 2)
    k_n = jnp.pad(k, pad_k)
    v_n = jnp.pad(v, pad_k)
    q_r = jnp.pad(q.reshape(db, s_new, KV_A, G_A, HD_A).transpose(0, 2, 3, 1, 4),
                  ((0, 0), (0, 0), (0, 0), (0, QPAD - s_new), (0, 0))).reshape(db, KV_A, G_A * QPAD, HD_A)

    def page_spec(rows_, pp, j):
        return pl.BlockSpec((None, None, rows_, LANES), lambda b, s, pt: (layer, pt[b, s * pp + j], 0, 0))

    cki = jnp.swapaxes(cache_kidx, 2, 3)
    ck = cache_k.reshape(cache_k.shape[:2] + (PAGE_SIZE * KV_A, HD_A))
    cv = cache_v.reshape(cache_v.shape[:2] + (PAGE_SIZE * KV_A, HD_A))

    bias, bias_new = pl.pallas_call(
        functools.partial(_ssi_kernel, pp=ppi, n_sel=n_sel, s_new=s_new, ns=nsi),
        grid_spec=pltpu.PrefetchScalarGridSpec(
            num_scalar_prefetch=1,
            grid=(db, nsi),
            in_specs=[
                pl.BlockSpec((None, QPAD * H_IDX, HD_IDX), lambda b, s, pt: (b, 0, 0)),
                pl.BlockSpec((None, QPAD * H_IDX, 1), lambda b, s, pt: (b, 0, 0)),
                pl.BlockSpec((None, HD_IDX, PAGE_SIZE), lambda b, s, pt: (b, 0, 0)),
            ] + [page_spec(HD_IDX, ppi, j) for j in range(ppi)],
            out_specs=[
                pl.BlockSpec((None, QPAD, n_pages * PAGE_SIZE), lambda b, s, pt: (b, 0, 0)),
                pl.BlockSpec((None, QPAD, PAGE_SIZE), lambda b, s, pt: (b, 0, 0)),
            ],
            scratch_shapes=[pltpu.VMEM((nsi, QPAD, ppi * PAGE_SIZE), jnp.int32)],
        ),
        out_shape=[jax.ShapeDtypeStruct((db, QPAD, n_pages * PAGE_SIZE), jnp.float32),
                   jax.ShapeDtypeStruct((db, QPAD, PAGE_SIZE), jnp.float32)],
        compiler_params=_params("parallel", "arbitrary"),
        name="sample_indexer",
    )(page_table, qi_r, w_r, ki_n, *([cki] * ppi))

    o = pl.pallas_call(
        functools.partial(_ssa_kernel, pp=ppa, ns=nsa),
        grid_spec=pltpu.PrefetchScalarGridSpec(
            num_scalar_prefetch=1,
            grid=(db, nsa),
            in_specs=[
                pl.BlockSpec((None, KV_A, G_A * QPAD, HD_A), lambda b, s, pt: (b, 0, 0, 0)),
                pl.BlockSpec((None, QPAD, ppa * PAGE_SIZE), lambda b, s, pt: (b, 0, s)),
                pl.BlockSpec((None, QPAD, PAGE_SIZE), lambda b, s, pt: (b, 0, 0)),
                pl.BlockSpec((None, PAGE_SIZE, kvw), lambda b, s, pt: (b, 0, 0)),
                pl.BlockSpec((None, PAGE_SIZE, kvw), lambda b, s, pt: (b, 0, 0)),
            ] + [page_spec(ck.shape[2], ppa, j) for j in range(ppa)] * 2,
            out_specs=pl.BlockSpec((None, rows, HD_A), lambda b, s, pt: (b, 0, 0)),
            scratch_shapes=[
                pltpu.VMEM((rows, 1), jnp.float32),
                pltpu.VMEM((rows, 1), jnp.float32),
                pltpu.VMEM((rows, HD_A), jnp.float32),
            ],
        ),
        out_shape=jax.ShapeDtypeStruct((db, rows, HD_A), jnp.float32),
        compiler_params=_params("parallel", "arbitrary"),
        name="sample_sparse_attn",
    )(page_table, q_r, bias, bias_new, k_n, v_n, *([ck] * ppa), *([cv] * ppa))
    o = o.reshape(db, KV_A, G_A, QPAD, HD_A)[:, :, :, :s_new].transpose(0, 3, 1, 2, 4)
    return o.reshape(db * s_new, H_A * HD_A).astype(jnp.bfloat16)


def _wa_kernel(q_ref, kp_ref, kc_ref, vp_ref, vc_ref, sink_ref, o_ref, *, tq, blocks_per_seq, mask_first):
    n = pl.program_id(0)
    nt = (((1,), (1,)), ((), ()))
    r = lax.broadcasted_iota(jnp.int32, (tq, 2 * WINDOW), 0)
    c = lax.broadcasted_iota(jnp.int32, (tq, 2 * WINDOW), 1)
    ok = (c - r >= 1) & (c - r <= WINDOW)
    if mask_first:
        ok = ok & ((n % blocks_per_seq != 0) | (c >= WINDOW))
    bias = jnp.where(ok, 0.0, NEG)
    bias = jnp.concatenate([bias] * G_B, axis=0)
    scale = HD_B ** -0.5
    for g in range(KV_B):
        cols = slice(g * HD_B, (g + 1) * HD_B)
        kcat = jnp.concatenate([kp_ref[:, cols], kc_ref[:, cols]], axis=0)
        vcat = jnp.concatenate([vp_ref[:, cols], vc_ref[:, cols]], axis=0)
        qg = jnp.concatenate(
            [q_ref[:, (g * G_B + j) * HD_B:(g * G_B + j + 1) * HD_B] for j in range(G_B)], axis=0)
        s = lax.dot_general(qg, kcat, nt, preferred_element_type=jnp.float32) * scale + bias
        sink = sink_ref[g]
        m = jnp.maximum(jnp.max(s, axis=1, keepdims=True), sink)
        p = jnp.exp(s - m)
        p = p / (jnp.sum(p, axis=1, keepdims=True) + jnp.exp(sink - m))
        og = jnp.dot(p.astype(jnp.bfloat16), vcat, preferred_element_type=jnp.float32)
        for j in range(G_B):
            h = g * G_B + j
            o_ref[:, h * HD_B:(h + 1) * HD_B] = og[j * tq:(j + 1) * tq].astype(o_ref.dtype)


def _wat_kernel(q_ref, kp_ref, kc_ref, vtp_ref, vtc_ref, sink_ref, o_ref, *, blocks_per_seq):
    tq = WINDOW
    n = pl.program_id(0)
    nt = (((1,), (1,)), ((), ()))
    kidx = lax.broadcasted_iota(jnp.int32, (2 * WINDOW, tq), 0)
    qidx = lax.broadcasted_iota(jnp.int32, (2 * WINDOW, tq), 1)
    ok = (kidx - qidx >= 1) & (kidx - qidx <= WINDOW) & ((n % blocks_per_seq != 0) | (kidx >= WINDOW))
    bias = jnp.where(ok, 0.0, NEG)
    bias = jnp.concatenate([bias] * G_B, axis=1)
    scale = HD_B ** -0.5
    for g in range(KV_B):
        cols = slice(g * HD_B, (g + 1) * HD_B)
        kcat = jnp.concatenate([kp_ref[:, cols], kc_ref[:, cols]], axis=0)
        vcat_t = jnp.concatenate([vtp_ref[cols, :], vtc_ref[cols, :]], axis=1)
        qg = jnp.concatenate(
            [q_ref[:, (g * G_B + j) * HD_B:(g * G_B + j + 1) * HD_B] for j in range(G_B)], axis=0)
        s = lax.dot_general(kcat, qg, nt, preferred_element_type=jnp.float32) * scale + bias
        sink = sink_ref[g]
        m = jnp.maximum(jnp.max(s, axis=0, keepdims=True), sink)
        p = jnp.exp(s - m)
        inv = 1.0 / (jnp.sum(p, axis=0, keepdims=True) + jnp.exp(sink - m))
        ot = jnp.dot(vcat_t, (p * inv).astype(jnp.bfloat16), preferred_element_type=jnp.float32)
        for j in range(0, G_B, 2):
            pair = jnp.concatenate([ot[:, j * tq:(j + 1) * tq], ot[:, (j + 1) * tq:(j + 2) * tq]], axis=0)
            h = g * G_B + j
            o_ref[:, h * HD_B:(h + 2) * HD_B] = pair.T.astype(o_ref.dtype)


def _window_attn_t(q, k, v_t, sinks, *, blocks_per_seq):
    tq = WINDOW
    nblk = q.shape[0] // tq
    kvw = KV_B * HD_B
    sink_rows = jnp.repeat(sinks.astype(jnp.float32).reshape(KV_B, G_B), tq, axis=1)[:, None, :]
    return pl.pallas_call(
        functools.partial(_wat_kernel, blocks_per_seq=blocks_per_seq),
        grid=(nblk,),
        in_specs=[
            pl.BlockSpec((tq, q.shape[1]), lambda n: (n, 0)),
            pl.BlockSpec((WINDOW, kvw), lambda n: (jnp.maximum(n - 1, 0), 0)),
            pl.BlockSpec((WINDOW, kvw), lambda n: (n, 0)),
            pl.BlockSpec((kvw, WINDOW), lambda n: (0, jnp.maximum(n - 1, 0))),
            pl.BlockSpec((kvw, WINDOW), lambda n: (0, n)),
            pl.BlockSpec((KV_B, 1, G_B * tq), lambda n: (0, 0, 0)),
        ],
        out_specs=pl.BlockSpec((tq, q.shape[1]), lambda n: (n, 0)),
        out_shape=jax.ShapeDtypeStruct(q.shape, jnp.bfloat16),
        compiler_params=_params("parallel"),
        name="window_attn_t",
    )(q, k, k, v_t, v_t, sink_rows)


def _window_attn(q, k_prev, k_cur, v_prev, v_cur, sinks, *, tq, blocks_per_seq, mask_first):
    nblk = q.shape[0] // tq
    kvw = KV_B * HD_B
    if mask_first:
        prev = lambda n: (jnp.maximum(n - 1, 0), 0)
    else:
        prev = lambda n: (n, 0)
    cur = lambda n: (n, 0)
    sink_rows = jnp.repeat(sinks.astype(jnp.float32).reshape(KV_B, G_B), tq, axis=1)[:, :, None]
    return pl.pallas_call(
        functools.partial(_wa_kernel, tq=tq, blocks_per_seq=blocks_per_seq, mask_first=mask_first),
        grid=(nblk,),
        in_specs=[
            pl.BlockSpec((tq, q.shape[1]), cur),
            pl.BlockSpec((WINDOW, kvw), prev),
            pl.BlockSpec((WINDOW, kvw), cur),
            pl.BlockSpec((WINDOW, kvw), prev),
            pl.BlockSpec((WINDOW, kvw), cur),
            pl.BlockSpec((KV_B, G_B * tq, 1), lambda n: (0, 0, 0)),
        ],
        out_specs=pl.BlockSpec((tq, q.shape[1]), cur),
        out_shape=jax.ShapeDtypeStruct(q.shape, jnp.bfloat16),
        compiler_params=_params("parallel"),
        name="window_attn",
    )(q, k_prev, k_cur, v_prev, v_cur, sink_rows)


def _trunk(x, c_mod, kv_mod, f_mod, pos, weights, *, tm, sparse_attn, make_ctx):
    (g_norm, w_in_a, w_out_a, g_kv, w_kv_b, w_q_b, sinks_b, w_out_b, w_gate_d, w_up_d, w_down_d,
     w_router, b_router, w_gate_e, w_up_e, w_down_e, g_final) = weights
    b, t, d = x.shape
    m = b * t
    per_row = tm > t
    tps = 1 if per_row else t // tm

    def seq(v):
        if per_row:
            return jnp.repeat(v, t, axis=0)[None]
        return v[:, None, :]

    pos_rows = jnp.tile(pos, b) if per_row else pos
    tab_a = _rope_table(pos_rows, HD_A, ROT_A)
    tab_i = _rope_table(pos_rows, HD_IDX, ROT_IDX)
    tab_t = _tail_table(pos_rows)
    tab_b = _rope_table(pos_rows, HD_B, ROT_B)

    xf = x.reshape(m, d)
    rows_a = []
    ctx = None
    for l in range(DEPTH):
        if l == N_A_LAYERS:
            shift, scale = jnp.split(kv_mod, 2, axis=-1)
            h = _adaln(xf, g_kv, seq(shift), seq(scale), tm=tm, tiles_per_seq=tps)
            kvw_b = KV_B * HD_B
            kb, kb_bf = _matmul(h, w_kv_b, col0=0, ncols=kvw_b, tm=tm, tn=kvw_b, rope=(tab_b, ROT_B // 2),
                                out_dtypes=(jnp.float32, jnp.bfloat16))
            vb, vb_bf = _matmul(h, w_kv_b, col0=kvw_b, ncols=kvw_b, tm=tm, tn=kvw_b,
                                out_dtypes=(jnp.float32, jnp.bfloat16))
            ctx = make_ctx(kb.reshape(b, t, KV_B, HD_B), vb.reshape(b, t, KV_B, HD_B), kb_bf, vb_bf)
        sh1, sc1, g1, sh2, sc2, g2 = jnp.split(c_mod[l], 6, axis=-1)
        h = _adaln(xf, g_norm[l, 0], seq(sh1), seq(sc1), tm=tm, tiles_per_seq=tps)
        if l < N_A_LAYERS:
            both = (jnp.float32, jnp.bfloat16)
            only_bf = (jnp.bfloat16,)
            kvw = KV_A * HD_A
            q = _matmul(h, w_in_a, layer=l, col0=0, ncols=QW_A, tm=tm, tn=512, rope=(tab_a, ROT_A // 2),
                        out_dtypes=only_bf)
            k, k_bf = _matmul(h, w_in_a, layer=l, col0=QW_A, ncols=kvw, tm=tm, tn=kvw,
                              rope=(tab_a, ROT_A // 2), out_dtypes=both)
            v, v_bf = _matmul(h, w_in_a, layer=l, col0=QW_A + kvw, ncols=kvw, tm=tm, tn=kvw, out_dtypes=both)
            qi = _matmul(h, w_in_a, layer=l, col0=QW_A + 2 * kvw, ncols=H_IDX * HD_IDX, tm=tm, tn=512,
                         rope=(tab_i, ROT_IDX // 2), out_dtypes=only_bf)
            w_tail = jnp.pad(w_in_a[l][:, QW_A + 2 * kvw + H_IDX * HD_IDX:],
                             ((0, 0), (0, LANES - HD_IDX - H_IDX)))
            tail = _matmul(h, w_tail, tm=tm, tn=LANES, rope=(tab_t, ROT_IDX // 2))
            o = sparse_attn(l, q, qi, k_bf, v_bf, tail)
            rows_a.append((k.reshape(b, t, KV_A, HD_A), v.reshape(b, t, KV_A, HD_A),
                           tail[:, :HD_IDX].reshape(b, t, HD_IDX)))
            w_o, lo = w_out_a, l
        else:
            lb = l - N_A_LAYERS
            q = _matmul(h, w_q_b, layer=lb, tm=tm, tn=512, rope=(tab_b, ROT_B // 2),
                        out_dtypes=(jnp.bfloat16,))
            o = ctx[0](q, sinks_b[lb])
            w_o, lo = w_out_b, lb
        xf = _matmul(o, w_o, layer=lo, tm=tm, tn=512, res=(xf, seq(g1), tps))
        if l % 2 == 0:
            h = _adaln(xf, g_norm[l, 1], seq(sh2), seq(sc2), tm=tm, tiles_per_seq=tps)
            xf = _ffn(h, w_gate_d, w_up_d, w_down_d, layer=l // 2, tm=tm, tf=256, res=(xf, seq(g2), tps))
        else:
            h, logits = _adaln(xf, g_norm[l, 1], seq(sh2), seq(sc2), tm=tm, tiles_per_seq=tps,
                               out_dtype=jnp.float32, router=(w_router[l // 2], b_router[l // 2]))
            tmc = min(tm, 256)
            xf = yield (l // 2, h, logits, (xf, seq(g2), tmc, 1 if per_row else t // tmc))
    fsh, fsc = jnp.split(f_mod, 2, axis=-1)
    y = _adaln(xf, g_final, seq(fsh), seq(fsc), tm=tm, tiles_per_seq=tps, out_dtype=jnp.float32)
    return y.reshape(b, t, d), rows_a, ctx


def kernel(x_prompt, x_sample, cache_k_a, cache_v_a, cache_kidx_a, state_k_b, state_v_b, page_table,
           c_prompt, c_sample, g_norm, w_mod, b_mod, w_in_a, w_out_a, g_kv, w_kvmod, b_kvmod, w_kv_b,
           w_q_b, sinks_b, w_out_b, w_gate_d, w_up_d, w_down_d, w_router, b_router, w_gate_e, w_up_e,
           w_down_e, g_final, w_fmod, b_fmod):
    n_pages = page_table.shape[1]
    past = n_pages * PAGE_SIZE
    nb_p, nb_s = c_prompt.shape[0], c_sample.shape[0]

    c_rows = 48
    c_all = jnp.zeros((c_rows, D_MODEL), jnp.bfloat16)
    c_all = c_all.at[:nb_p + nb_s].set(jnp.concatenate([c_prompt, c_sample], axis=0).astype(jnp.bfloat16))
    mods = [_matmul(c_all, w_mod, layer=l, tm=c_rows, tn=512, bias=b_mod[l][None]) for l in range(DEPTH)]
    kvm = _matmul(c_all, w_kvmod, tm=c_rows, tn=512, bias=b_kvmod[None])
    fm = _matmul(c_all, w_fmod, tm=c_rows, tn=512, bias=b_fmod[None])

    sb, st = x_sample.shape[:2]
    w_buf = state_k_b.shape[1]
    kvw_b = KV_B * HD_B

    def prompt_sparse_attn(li, q, qi, k_bf, v_bf, tail):
        b, t = x_prompt.shape[:2]
        return _prompt_sparse_attn_t(q, qi, k_bf, v_bf, tail, b=b, t=t, tq=256, tk=256,
                                   n_sel=min(TOPK_MAX, t // 4))

    def sample_sparse_attn(li, q, qi, k_bf, v_bf, tail):
        r3 = lambda a: a.reshape(sb, st, a.shape[-1])
        return _sample_sparse_attn(
            r3(q), r3(qi), r3(tail[:, HD_IDX:HD_IDX + H_IDX]), r3(tail[:, :HD_IDX]), r3(k_bf), r3(v_bf),
            cache_k_a, cache_v_a, cache_kidx_a, page_table, layer=li, n_sel=min(TOPK_MAX, (past + st) // 4))

    def prompt_ctx(kb, vb, kb_bf, vb_bf):
        t = kb.shape[1]
        wp = min(WINDOW, t)

        vb_t = vb_bf.T

        def attend(q, sinks):
            return _window_attn_t(q, kb_bf, vb_t, sinks, blocks_per_seq=t // WINDOW)

        return attend, kb[:, t - wp:], vb[:, t - wp:]

    def sample_ctx(kb, vb, kb_bf, vb_bf):
        tq = 16
        pad_new = lambda a: jnp.pad(a.reshape(sb, st, kvw_b), ((0, 0), (0, WINDOW - st), (0, 0))).reshape(-1, kvw_b)
        k_prev = state_k_b.astype(jnp.bfloat16).reshape(sb * w_buf, kvw_b)
        v_prev = state_v_b.astype(jnp.bfloat16).reshape(sb * w_buf, kvw_b)
        k_cur, v_cur = pad_new(kb_bf), pad_new(vb_bf)

        def attend(q, sinks):
            qp = jnp.pad(q.reshape(sb, st, -1), ((0, 0), (0, tq - st), (0, 0))).reshape(sb * tq, -1)
            o = _window_attn(qp, k_prev, k_cur, v_prev, v_cur, sinks, tq=tq, blocks_per_seq=1, mask_first=False)
            return o.reshape(sb, tq, -1)[:, :st].reshape(sb * st, -1)

        k_ctx = jnp.concatenate([state_k_b, kb], axis=1)
        v_ctx = jnp.concatenate([state_v_b, vb], axis=1)
        return attend, k_ctx[:, -w_buf:], v_ctx[:, -w_buf:]

    weights = (g_norm, w_in_a, w_out_a, g_kv, w_kv_b, w_q_b, sinks_b, w_out_b, w_gate_d, w_up_d, w_down_d,
               w_router, b_router, w_gate_e, w_up_e, w_down_e, g_final)
    pos_p = jnp.arange(x_prompt.shape[1])
    pos_s = past + jnp.arange(x_sample.shape[1])
    trunks = [
        _trunk(x_prompt, [mm[:nb_p] for mm in mods], kvm[:nb_p], fm[:nb_p], pos_p, weights,
               tm=1024, sparse_attn=prompt_sparse_attn, make_ctx=prompt_ctx),
        _trunk(x_sample, [mm[nb_p:nb_p + nb_s] for mm in mods], kvm[nb_p:nb_p + nb_s], fm[nb_p:nb_p + nb_s], pos_s,
               weights, tm=sb * st, sparse_attn=sample_sparse_attn, make_ctx=sample_ctx),
    ]
    requests = [next(tr) for tr in trunks]
    results = [None] * len(trunks)
    while any(r is None for r in results):
        layer = requests[0][0]
        new_x = _moe([r[1] for r in requests], [r[2] for r in requests], w_gate_e, w_up_e, w_down_e,
                     layer=layer, tm=MOE_ROW_TILE, tf=256, res_list=[r[3] for r in requests])
        for n, (tr, xn) in enumerate(zip(trunks, new_x)):
            try:
                requests[n] = tr.send(xn)
            except StopIteration as done:
                results[n] = done.value
    (y_prompt, rows_p, ctx_p), (y_sample, rows_s, ctx_s) = results

    return (y_prompt, y_sample,
            jnp.stack([r[0] for r in rows_p]), jnp.stack([r[1] for r in rows_p]), jnp.stack([r[2] for r in rows_p]),
            jnp.stack([r[0] for r in rows_s]), jnp.stack([r[1] for r in rows_s]), jnp.stack([r[2] for r in rows_s]),
            ctx_p[1], ctx_p[2], ctx_s[1], ctx_s[2])
```

```python
import functools
import math

import jax
import jax.numpy as jnp
import numpy as np
from jax import lax
from jax.experimental import pallas as pl
from jax.experimental.pallas import tpu as pltpu

D_MODEL = 2048
DEPTH = 4
PAGE_SIZE = 128
HD_A = 128
H_A = D_MODEL // HD_A
KV_A = 4
G_A = H_A // KV_A
QW_A = H_A * HD_A
H_IDX = 16
HD_IDX = 64
TOPK_MAX = 256
IDX_W_SCALE = (H_IDX * HD_IDX) ** -0.5
Q_BLOCK = 128
HD_B = 64
H_B = D_MODEL // HD_B
KV_B = 4
G_B = H_B // KV_B
QW_B = H_B * HD_B
KVW_B = 2 * KV_B * HD_B
WINDOW = 128
ROPE_THETA = 500000.0
ROT_A = HD_A // 4
ROT_IDX = HD_IDX // 4
ROT_B = HD_B // 4
N_EXPERTS = 8
TOP_K_EXPERTS = 2
EPS = 1e-6
N_A_LAYERS = DEPTH // 2

LANES = 128
VMEM_LIMIT = 56 * 1024 * 1024
MOE_ROW_TILE = 2304


def _params(*sem):
    return pltpu.CompilerParams(dimension_semantics=sem, vmem_limit_bytes=VMEM_LIMIT)


def _rope_table(pos, head_dim, rot):
    half = rot // 2
    inv = jnp.exp(-math.log(ROPE_THETA) * jnp.arange(half, dtype=jnp.float32) / half)
    ang = pos.astype(jnp.float32)[:, None] * inv[None, :]
    cos, sin = jnp.cos(ang), jnp.sin(ang)
    d = np.arange(LANES) % head_dim
    idx = np.where(d < half, d, np.clip(d - half, 0, half - 1))
    cos_l, sin_l = cos[:, idx], sin[:, idx]
    c = jnp.where((d < rot)[None], cos_l, 1.0)
    u = jnp.where(((d >= half) & (d < rot))[None], sin_l, 0.0)
    dn = jnp.where((d < half)[None], -sin_l, 0.0)
    return jnp.concatenate([c, u, dn], axis=1)


def _tail_table(pos):
    t = _rope_table(pos, HD_IDX, ROT_IDX)
    lane = np.arange(LANES)
    c = jnp.where((lane < HD_IDX)[None], t[:, :LANES],
                  jnp.where((lane < HD_IDX + H_IDX)[None], IDX_W_SCALE, 1.0))
    keep = (lane < HD_IDX)[None]
    u = jnp.where(keep, t[:, LANES:2 * LANES], 0.0)
    dn = jnp.where(keep, t[:, 2 * LANES:], 0.0)
    return jnp.concatenate([c, u, dn], axis=1)


def _adaln_kernel(*refs, has_router):
    x_ref, g_ref, sh_ref, sc_ref = refs[:4]
    x = x_ref[...]
    y = x * lax.rsqrt(jnp.mean(x * x, axis=-1, keepdims=True) + EPS)
    h = (y * g_ref[...]) * (1.0 + sc_ref[0]) + sh_ref[0]
    if has_router:
        wr_ref, br_ref, h_ref, lg_ref = refs[4:]
        h_ref[...] = h.astype(h_ref.dtype)
        lg_ref[...] = jnp.dot(h.astype(jnp.bfloat16), wr_ref[...], preferred_element_type=jnp.float32) + br_ref[...]
    else:
        h_ref = refs[4]
        h_ref[...] = h.astype(h_ref.dtype)


def _adaln(x, g, shift, scale, *, tm, tiles_per_seq, out_dtype=jnp.bfloat16, router=None):
    m, d = x.shape
    r = shift.shape[1]
    in_specs = [
        pl.BlockSpec((tm, d), lambda i: (i, 0)),
        pl.BlockSpec((1, d), lambda i: (0, 0)),
        pl.BlockSpec((1, r, d), lambda i: (i // tiles_per_seq, 0, 0)),
        pl.BlockSpec((1, r, d), lambda i: (i // tiles_per_seq, 0, 0)),
    ]
    args = [x, g.reshape(1, d), shift, scale]
    out_shape = [jax.ShapeDtypeStruct((m, d), out_dtype)]
    out_specs = [pl.BlockSpec((tm, d), lambda i: (i, 0))]
    if router is not None:
        w_r, b_r = router
        wr = jnp.zeros((d, LANES), jnp.bfloat16).at[:, :N_EXPERTS].set(w_r.astype(jnp.bfloat16))
        br = jnp.zeros((1, LANES), jnp.float32).at[0, :N_EXPERTS].set(b_r.astype(jnp.float32))
        in_specs += [pl.BlockSpec((d, LANES), lambda i: (0, 0)), pl.BlockSpec((1, LANES), lambda i: (0, 0))]
        args += [wr, br]
        out_shape.append(jax.ShapeDtypeStruct((m, LANES), jnp.float32))
        out_specs.append(pl.BlockSpec((tm, LANES), lambda i: (i, 0)))
    outs = pl.pallas_call(
        functools.partial(_adaln_kernel, has_router=router is not None),
        grid=(m // tm,),
        in_specs=in_specs,
        out_specs=out_specs,
        out_shape=out_shape,
        compiler_params=_params("parallel"),
        name="adaln",
    )(*args)
    return outs if router is not None else outs[0]


def _mm_kernel(*refs, tn, has_bias, rope_half, has_res):
    it = iter(refs)
    a_ref = next(it)
    w_ref = next(it)
    bias_ref = next(it) if has_bias else None
    tab_ref = next(it) if rope_half else None
    res_ref = next(it) if has_res else None
    gate_ref = next(it) if has_res else None
    out_refs = list(it)
    acc = jnp.dot(a_ref[...], w_ref[...].astype(jnp.bfloat16), preferred_element_type=jnp.float32)
    if has_bias:
        acc = acc + bias_ref[...]
    if rope_half:
        c = tab_ref[:, :LANES]
        u = tab_ref[:, LANES:2 * LANES]
        dn = tab_ref[:, 2 * LANES:]
        parts = []
        for grp in range(tn // LANES):
            xg = acc[:, grp * LANES:(grp + 1) * LANES]
            parts.append(xg * c + pltpu.roll(xg, rope_half, 1) * u + pltpu.roll(xg, LANES - rope_half, 1) * dn)
        acc = parts[0] if len(parts) == 1 else jnp.concatenate(parts, axis=1)
    if has_res:
        acc = res_ref[...] + gate_ref[0] * acc
    for o in out_refs:
        o[...] = acc.astype(o.dtype)


def _matmul(a, w, *, layer=0, col0=0, ncols=None, tm, tn, bias=None, rope=None, res=None,
            out_dtypes=(jnp.float32,)):
    if w.ndim == 2:
        w = w[None]
    m, k = a.shape
    ncols = w.shape[2] - col0 if ncols is None else ncols
    assert ncols % tn == 0 and col0 % tn == 0 and m % tm == 0
    cb = col0 // tn
    in_specs = [
        pl.BlockSpec((tm, k), lambda i, j: (i, 0)),
        pl.BlockSpec((None, k, tn), lambda i, j: (layer, 0, cb + j)),
    ]
    args = [a, w]
    if bias is not None:
        in_specs.append(pl.BlockSpec((1, tn), lambda i, j: (0, cb + j)))
        args.append(bias)
    rope_half = 0
    if rope is not None:
        tab, rope_half = rope
        tab_tiles = tab.shape[0] // tm
        in_specs.append(pl.BlockSpec((tm, 3 * LANES), lambda i, j: (i % tab_tiles, 0)))
        args.append(tab)
    if res is not None:
        resid, gate, tps = res
        r = gate.shape[1]
        in_specs.append(pl.BlockSpec((tm, tn), lambda i, j: (i, j)))
        in_specs.append(pl.BlockSpec((1, r, tn), lambda i, j: (i // tps, 0, j)))
        args += [resid, gate]
    outs = pl.pallas_call(
        functools.partial(_mm_kernel, tn=tn, has_bias=bias is not None, rope_half=rope_half,
                          has_res=res is not None),
        grid=(m // tm, ncols // tn),
        in_specs=in_specs,
        out_specs=[pl.BlockSpec((tm, tn), lambda i, j: (i, j)) for _ in out_dtypes],
        out_shape=[jax.ShapeDtypeStruct((m, ncols), dt) for dt in out_dtypes],
        compiler_params=_params("parallel", "arbitrary"),
        name="matmul",
    )(*args)
    return outs if len(outs) > 1 else outs[0]


def _start_row_gather(idx_ref, base, src_hbm, dst, sem, n_rows):
    def issue(r, carry):
        pltpu.make_async_copy(src_hbm.at[pl.ds(idx_ref[base + r], 1)], dst.at[pl.ds(r, 1)], sem).start()
        return carry

    lax.fori_loop(0, n_rows, issue, 0, unroll=8)


def _wait_row_gather(src_hbm, dst, sem, n_rows):
    pltpu.make_async_copy(src_hbm.at[pl.ds(0, n_rows)], dst, sem).wait()


def _ffn_kernel(te_ref, tv_ref, rt_ref, *refs, nj, mode, tm):
    del te_ref
    if mode == "dense":
        x_ref, wg_ref, wu_ref, wd_ref, res_ref, gate_ref, o_ref = refs
    else:
        x_hbm, wg_ref, wu_ref, wd_ref, rs_ref, o_ref, x_ref, sem = refs
    i = pl.program_id(0)
    j = pl.program_id(1)

    if mode == "routed":
        @pl.when((j == 0) & (tv_ref[i] > 0))
        def _():
            _start_row_gather(rt_ref, i * tm, x_hbm, o_ref, sem, tm)
            _wait_row_gather(x_hbm, o_ref, sem, tm)
            x_ref[...] = o_ref[...].astype(x_ref.dtype)

    @pl.when(j == 0)
    def _():
        o_ref[...] = jnp.zeros_like(o_ref)

    @pl.when(tv_ref[i] > 0)
    def _():
        x = x_ref[...]
        g = jnp.dot(x, wg_ref[...].astype(jnp.bfloat16), preferred_element_type=jnp.float32)
        u = jnp.dot(x, wu_ref[...].astype(jnp.bfloat16), preferred_element_type=jnp.float32)
        act = ((g * (1.0 / (1.0 + jnp.exp(-g)))) * u).astype(jnp.bfloat16)
        d = o_ref.shape[1]
        cw = min(d, 512)
        for c0 in range(0, d, cw):
            o_ref[:, c0:c0 + cw] += jnp.dot(act, wd_ref[:, c0:c0 + cw].astype(jnp.bfloat16),
                                            preferred_element_type=jnp.float32)

    @pl.when(j == nj - 1)
    def _():
        if mode == "dense":
            o_ref[...] = res_ref[...] + gate_ref[0] * o_ref[...]
        else:
            o_ref[...] = rs_ref[...] * o_ref[...]


def _ffn(x, wg, wu, wd, *, layer, tm, tf, tile_expert=None, tile_valid=None, row_token=None, res=None,
         row_scale=None):
    d = x.shape[1]
    f = wg.shape[-1]
    dense = res is not None
    m = x.shape[0] if dense else row_token.shape[0]
    nt, nj = m // tm, f // tf
    if dense:
        tile_expert = jnp.full((nt,), layer, jnp.int32)
        tile_valid = jnp.ones((nt,), jnp.int32)
        row_token = jnp.zeros((1,), jnp.int32)
    else:
        tile_expert = tile_expert + layer * wg.shape[1]
        wg, wu, wd = wg.reshape(-1, d, f), wu.reshape(-1, d, f), wd.reshape(-1, f, d)

    def jj(i, j, tv):
        return jnp.where(tv[i] > 0, j, nj - 1)

    in_specs = [
        pl.BlockSpec((tm, d), lambda i, j, te, tv, rt: (i, 0)) if dense else pl.BlockSpec(memory_space=pl.ANY),
        pl.BlockSpec((None, d, tf), lambda i, j, te, tv, rt: (te[i], 0, jj(i, j, tv))),
        pl.BlockSpec((None, d, tf), lambda i, j, te, tv, rt: (te[i], 0, jj(i, j, tv))),
        pl.BlockSpec((None, tf, d), lambda i, j, te, tv, rt: (te[i], jj(i, j, tv), 0)),
    ]
    args = [x, wg, wu, wd]
    scratch = []
    if dense:
        resid, gate, tps = res
        r = gate.shape[1]
        in_specs.append(pl.BlockSpec((tm, d), lambda i, j, te, tv, rt: (i, 0), pipeline_mode=pl.Buffered(1)))
        in_specs.append(pl.BlockSpec((1, r, d), lambda i, j, te, tv, rt: (i // tps, 0, 0)))
        args += [resid, gate]
    else:
        in_specs.append(pl.BlockSpec((tm, 1), lambda i, j, te, tv, rt: (i, 0)))
        args.append(row_scale)
        scratch = [pltpu.VMEM((tm, d), jnp.bfloat16), pltpu.SemaphoreType.DMA(())]
    out_mode = {} if dense else {"pipeline_mode": pl.Buffered(1)}
    return pl.pallas_call(
        functools.partial(_ffn_kernel, nj=nj, mode="dense" if dense else "routed", tm=tm),
        grid_spec=pltpu.PrefetchScalarGridSpec(
            num_scalar_prefetch=3,
            grid=(nt, nj),
            in_specs=in_specs,
            out_specs=pl.BlockSpec((tm, d), lambda i, j, te, tv, rt: (i, 0), **out_mode),
            scratch_shapes=scratch,
        ),
        out_shape=jax.ShapeDtypeStruct((m, d), jnp.float32),
        compiler_params=_params("parallel" if dense else "arbitrary", "arbitrary"),
        name="ffn_dense" if dense else "ffn_routed",
    )(tile_expert, tile_valid, row_token, *args)


def _combine_kernel(dest_ref, ys_hbm, res_ref, gate_ref, o_ref, buf, sem, *, tm):
    i = pl.program_id(0)
    npair_half = pl.num_programs(0) * tm

    def issue(r, carry):
        for k in range(TOP_K_EXPERTS):
            row = dest_ref[k * npair_half + i * tm + r]
            pltpu.make_async_copy(ys_hbm.at[pl.ds(row, 1)], buf.at[pl.ds(k * tm + r, 1)], sem).start()
        return carry

    lax.fori_loop(0, tm, issue, 0, unroll=4)
    pltpu.make_async_copy(ys_hbm.at[pl.ds(0, TOP_K_EXPERTS * tm)], buf, sem).wait()
    f = buf[0:tm]
    for k in range(1, TOP_K_EXPERTS):
        f = f + buf[k * tm:(k + 1) * tm]
    o_ref[...] = res_ref[...] + gate_ref[0] * f


def _moe_combine(ys, dest, res, gate, *, tm, tiles_per_seq):
    n, d = res.shape
    r = gate.shape[1]
    dest_km = dest.reshape(n, TOP_K_EXPERTS).T.reshape(-1)
    return pl.pallas_call(
        functools.partial(_combine_kernel, tm=tm),
        grid_spec=pltpu.PrefetchScalarGridSpec(
            num_scalar_prefetch=1,
            grid=(n // tm,),
            in_specs=[
                pl.BlockSpec(memory_space=pl.ANY),
                pl.BlockSpec((tm, d), lambda i, dr: (i, 0)),
                pl.BlockSpec((1, r, d), lambda i, dr: (i // tiles_per_seq, 0, 0)),
            ],
            out_specs=pl.BlockSpec((tm, d), lambda i, dr: (i, 0)),
            scratch_shapes=[pltpu.VMEM((TOP_K_EXPERTS * tm, d), jnp.float32), pltpu.SemaphoreType.DMA(())],
        ),
        out_shape=jax.ShapeDtypeStruct((n, d), jnp.float32),
        compiler_params=_params("arbitrary"),
        name="moe_combine",
    )(dest_km, ys, res, gate)


def _moe(hs, logits_list, wg, wu, wd, *, layer, tm, tf, res_list):
    h = jnp.concatenate(hs, axis=0) if len(hs) > 1 else hs[0]
    logits = jnp.concatenate(logits_list, axis=0) if len(hs) > 1 else logits_list[0]
    n, d = h.shape
    vals, idx = lax.top_k(logits[:, :N_EXPERTS], TOP_K_EXPERTS)
    gates = jax.nn.softmax(vals, axis=-1)
    npair = n * TOP_K_EXPERTS
    nt = npair // tm + N_EXPERTS
    flat_e = idx.reshape(-1).astype(jnp.int32)
    order = jnp.argsort(flat_e, stable=True)
    onehot = (flat_e[:, None] == jnp.arange(N_EXPERTS)[None, :]).astype(jnp.int32)
    counts = jnp.sum(onehot, axis=0)
    rank = jnp.take_along_axis(jnp.cumsum(onehot, axis=0), flat_e[:, None], axis=1)[:, 0] - 1
    tiles_e = (counts + tm - 1) // tm
    tile_end = jnp.cumsum(tiles_e)
    tile_start = tile_end - tiles_e
    grp_start = jnp.cumsum(counts) - counts
    dest = tile_start[flat_e] * tm + rank
    tile_ids = jnp.arange(nt, dtype=jnp.int32)
    total = tile_end[-1]
    tile_valid = (tile_ids < total).astype(jnp.int32)
    last_e = jnp.searchsorted(tile_end, total - 1, side="right").astype(jnp.int32)
    tile_expert = jnp.where(tile_valid > 0, jnp.searchsorted(tile_end, tile_ids, side="right").astype(jnp.int32),
                            last_e)
    tile_expert = jnp.clip(tile_expert, 0, N_EXPERTS - 1)
    row_ids = jnp.arange(nt * tm, dtype=jnp.int32)
    e_row = tile_expert[row_ids // tm]
    rank_row = row_ids - tile_start[e_row] * tm
    live = (tile_valid[row_ids // tm] > 0) & (rank_row < counts[e_row])
    pair = order[jnp.clip(grp_start[e_row] + rank_row, 0, npair - 1)]
    row_token = jnp.where(live, pair // TOP_K_EXPERTS, 0).astype(jnp.int32)
    row_gate = jnp.where(live, gates.reshape(-1)[pair], 0.0)
    ys = _ffn(h, wg, wu, wd, layer=layer, tm=tm, tf=tf, tile_expert=tile_expert, tile_valid=tile_valid,
              row_token=row_token, row_scale=row_gate[:, None])
    outs = []
    start = 0
    for hh, (resid, gate, tmc, tps) in zip(hs, res_list):
        stop = start + hh.shape[0] * TOP_K_EXPERTS
        outs.append(_moe_combine(ys, dest[start:stop], resid, gate, tm=tmc, tiles_per_seq=tps))
        start = stop
    return outs


INT_MIN = -(2 ** 31)
NEG = -1e30


def _sortable(score):
    bits = pltpu.bitcast(score, jnp.int32)
    return bits ^ ((bits >> 31) & jnp.int32(0x7FFFFFFF))


def _kth_largest(count_ge, shape, n_sel):
    def step(s, t):
        cand = t + lax.shift_left(jnp.int32(1), 31 - s)
        return jnp.where(count_ge(cand) >= n_sel, cand, t)

    return lax.fori_loop(0, 32, step, jnp.full(shape, INT_MIN, jnp.int32))


def _psat_kernel(q_ref, qi_ref, k_ref, vt_ref, tail_ref, wt_ref, o_ref, key_scr, m_scr, l_scr, acc_scr, *,
                 tq, tk, n_sel):
    i = pl.program_id(1)
    nck = (i + 1) * (tq // tk)
    row0 = i * tq
    nt = (((1,), (1,)), ((), ()))
    lane = lax.broadcasted_iota(jnp.int32, (tk, LANES), 1)
    kidx0 = lax.broadcasted_iota(jnp.int32, (tk, tq), 0)
    qidx = row0 + lax.broadcasted_iota(jnp.int32, (tk, tq), 1)

    def score_chunk(c, carry):
        ks = pl.multiple_of(c * tk, tk)
        kf = jnp.where(lane < HD_IDX, tail_ref[pl.ds(ks, tk), :], 0.0)
        k_even = kf.astype(jnp.bfloat16)
        k_odd = pltpu.roll(kf, HD_IDX, 1).astype(jnp.bfloat16)
        sc = jnp.zeros((tk, tq), jnp.float32)
        for h in range(H_IDX):
            qp = qi_ref[:, (h // 2) * LANES:(h // 2 + 1) * LANES]
            d = lax.dot_general(k_even if h % 2 == 0 else k_odd, qp, nt, preferred_element_type=jnp.float32)
            sc = sc + jnp.maximum(d, 0.0) * wt_ref[h:h + 1, :]
        key_scr[c] = jnp.where(ks + kidx0 <= qidx, _sortable(sc), INT_MIN)
        return carry

    lax.fori_loop(0, nck, score_chunk, 0)

    def count_ge(cand):
        def chunk(c, acc):
            hit = jnp.where(key_scr[c] >= cand, 1.0, 0.0)
            return acc + jnp.sum(hit.reshape(tk // 8, 8, tq), axis=0)

        acc = lax.fori_loop(0, nck, chunk, jnp.zeros((8, tq), jnp.float32))
        return jnp.sum(acc, axis=0, keepdims=True)

    thr = jnp.maximum(_kth_largest(count_ge, (1, tq), float(n_sel)), INT_MIN + 1)

    need = float(n_sel) - count_ge(thr + 1)
    tri = (lax.broadcasted_iota(jnp.int32, (tk, tk), 0) >= lax.broadcasted_iota(jnp.int32, (tk, tk), 1))
    tri = jnp.where(tri, 1.0, 0.0).astype(jnp.bfloat16)

    def select_chunk(c, ties_before):
        kk = key_scr[c]
        eq = kk == thr
        rank = jnp.dot(tri, jnp.where(eq, 1.0, 0.0).astype(jnp.bfloat16),
                       preferred_element_type=jnp.float32) + ties_before
        key_scr[c] = jnp.where((kk > thr) | (eq & (rank <= need)), 1, 0)
        return rank[tk - 1:tk, :]

    lax.fori_loop(0, nck, select_chunk, jnp.zeros((1, tq), jnp.float32))

    m_scr[...] = jnp.full_like(m_scr, NEG)
    l_scr[...] = jnp.zeros_like(l_scr)
    acc_scr[...] = jnp.zeros_like(acc_scr)
    c_exp = (HD_A ** -0.5) * math.log2(math.e)

    def att_chunk(c, carry):
        ks = pl.multiple_of(c * tk, tk)
        bias = jnp.where(key_scr[c] > 0, 0.0, NEG)
        bias = jnp.concatenate([bias] * G_A, axis=1)
        for g in range(KV_A):
            qg = jnp.concatenate(
                [q_ref[:, (g * G_A + j) * HD_A:(g * G_A + j + 1) * HD_A] for j in range(G_A)], axis=0)
            kg = k_ref[pl.ds(ks, tk), g * HD_A:(g + 1) * HD_A]
            s = lax.dot_general(kg, qg, nt, preferred_element_type=jnp.float32) + bias
            m_old = m_scr[g]
            m_new = jnp.maximum(m_old, jnp.max(s, axis=0, keepdims=True))
            alpha = jnp.exp2((m_old - m_new) * c_exp)
            p = jnp.exp2((s - m_new) * c_exp)
            l_scr[g] = alpha * l_scr[g] + jnp.sum(p, axis=0, keepdims=True)
            vtg = vt_ref[c, g * HD_A:(g + 1) * HD_A, :]
            acc_scr[g] = alpha * acc_scr[g] + jnp.dot(vtg, p.astype(jnp.bfloat16),
                                                      preferred_element_type=jnp.float32)
            m_scr[g] = m_new
        return carry

    lax.fori_loop(0, nck, att_chunk, 0)

    for g in range(KV_A):
        og = acc_scr[g] * (1.0 / l_scr[g])
        for j in range(G_A):
            h = g * G_A + j
            o_ref[:, h * HD_A:(h + 1) * HD_A] = og[:, j * tq:(j + 1) * tq].T.astype(o_ref.dtype)


def _prompt_sparse_attn_t(q, qi, k, v, tail, *, b, t, tq, tk, n_sel):
    nqt = t // tq
    nck = t // tk
    kvw = v.shape[1]
    vt = v.reshape(b, nck, tk, kvw).transpose(0, 1, 3, 2)
    wt = tail[:, HD_IDX:HD_IDX + H_IDX].T
    return pl.pallas_call(
        functools.partial(_psat_kernel, tq=tq, tk=tk, n_sel=n_sel),
        grid=(b, nqt),
        in_specs=[
            pl.BlockSpec((tq, q.shape[1]), lambda bb, i: (bb * nqt + i, 0)),
            pl.BlockSpec((tq, qi.shape[1]), lambda bb, i: (bb * nqt + i, 0)),
            pl.BlockSpec((t, k.shape[1]), lambda bb, i: (bb, 0)),
            pl.BlockSpec((None, nck, kvw, tk), lambda bb, i: (bb, 0, 0, 0)),
            pl.BlockSpec((t, LANES), lambda bb, i: (bb, 0)),
            pl.BlockSpec((H_IDX, tq), lambda bb, i: (0, bb * nqt + i)),
        ],
        out_specs=pl.BlockSpec((tq, q.shape[1]), lambda bb, i: (bb * nqt + i, 0)),
        out_shape=jax.ShapeDtypeStruct(q.shape, jnp.bfloat16),
        scratch_shapes=[
            pltpu.VMEM((nck, tk, tq), jnp.int32),
            pltpu.VMEM((KV_A, 1, G_A * tq), jnp.float32),
            pltpu.VMEM((KV_A, 1, G_A * tq), jnp.float32),
            pltpu.VMEM((KV_A, HD_A, G_A * tq), jnp.float32),
        ],
        compiler_params=_params("parallel", "arbitrary"),
        name="prompt_sparse_attn_t",
    )(q, qi, k, vt, tail, wt)


QPAD = 8
PAGES_PER_STEP = 16
PAGES_PER_INDEX_STEP = 32


def _ssi_kernel(pt_ref, qi_ref, w_ref, kn_ref, *refs, pp, n_sel, s_new, ns):
    del pt_ref
    page_refs = refs[:pp]
    bias_ref, bias_new_ref, key_scr = refs[pp:]
    s = pl.program_id(1)
    n = pp * PAGE_SIZE

    def scores(kt):
        d = jnp.dot(qi_ref[...], kt, preferred_element_type=jnp.float32)
        sc = jnp.maximum(d, 0.0) * w_ref[...]
        return jnp.sum(sc.reshape(QPAD, H_IDX, kt.shape[1]), axis=1)

    kt = jnp.concatenate([r[...] for r in page_refs], axis=1).astype(jnp.bfloat16)
    qrow = lax.broadcasted_iota(jnp.int32, (QPAD, n), 0)
    key_scr[s] = jnp.where(qrow < s_new, _sortable(scores(kt)), INT_MIN)

    @pl.when(s == ns - 1)
    def _():
        qr = lax.broadcasted_iota(jnp.int32, (QPAD, PAGE_SIZE), 0)
        col = lax.broadcasted_iota(jnp.int32, (QPAD, PAGE_SIZE), 1)
        key_new = jnp.where((col <= qr) & (qr < s_new), _sortable(scores(kn_ref[...])), INT_MIN)

        def count_ge(cand):
            acc = jnp.where(key_new >= cand, 1.0, 0.0)
            for c in range(ns):
                hit = jnp.where(key_scr[c] >= cand, 1.0, 0.0)
                for u in range(n // LANES):
                    acc = acc + hit[:, u * LANES:(u + 1) * LANES]
            return jnp.sum(acc, axis=1, keepdims=True)

        thr = jnp.maximum(_kth_largest(count_ge, (QPAD, 1), float(n_sel)), INT_MIN + 1)

        need = float(n_sel) - count_ge(thr + 1)
        nblk = n // LANES
        r_i = lax.broadcasted_iota(jnp.int32, (nblk * QPAD, nblk * QPAD), 0)
        c_i = lax.broadcasted_iota(jnp.int32, (nblk * QPAD, nblk * QPAD), 1)
        earlier = jnp.where((r_i % QPAD == c_i % QPAD) & (c_i // QPAD < r_i // QPAD), 1.0, 0.0).astype(jnp.bfloat16)
        upto = lax.broadcasted_iota(jnp.int32, (LANES, LANES), 0) <= lax.broadcasted_iota(jnp.int32, (LANES, LANES), 1)
        upto = jnp.where(upto, 1.0, 0.0).astype(jnp.bfloat16)

        def select(kk, ties_before, blocks):
            rep = lambda a: jnp.concatenate([a] * blocks, axis=0) if blocks > 1 else a
            eq = kk == rep(thr)
            rank = jnp.dot(jnp.where(eq, 1.0, 0.0).astype(jnp.bfloat16), upto, preferred_element_type=jnp.float32)
            if blocks > 1:
                rank = rank + jnp.dot(earlier, rank.astype(jnp.bfloat16),
                                      preferred_element_type=jnp.float32)[:, LANES - 1:LANES]
            rank = rank + rep(ties_before)
            sel = (kk > rep(thr)) | (eq & (rank <= rep(need)))
            return jnp.where(sel, 0.0, NEG), rank[(blocks - 1) * QPAD:, LANES - 1:LANES]

        ties = jnp.zeros((QPAD, 1), jnp.float32)
        for c in range(ns):
            kc = key_scr[c]
            stacked = jnp.concatenate([kc[:, u * LANES:(u + 1) * LANES] for u in range(nblk)], axis=0)
            bias_c, ties = select(stacked, ties, nblk)
            for u in range(nblk):
                bias_ref[:, c * n + u * LANES:c * n + (u + 1) * LANES] = bias_c[u * QPAD:(u + 1) * QPAD]
        bias_new_ref[...] = select(key_new, ties, 1)[0]


def _ssa_kernel(pt_ref, q_ref, bias_ref, bias_new_ref, kn_ref, vn_ref, *refs, pp, ns):
    del pt_ref
    k_refs = refs[:pp]
    v_refs = refs[pp:2 * pp]
    o_ref, m_scr, l_scr, acc_scr = refs[2 * pp:]
    s = pl.program_id(1)
    nt = (((1,), (1,)), ((), ()))
    scale = HD_A ** -0.5

    @pl.when(s == 0)
    def _():
        m_scr[...] = jnp.full_like(m_scr, NEG)
        l_scr[...] = jnp.zeros_like(l_scr)
        acc_scr[...] = jnp.zeros_like(acc_scr)

    rows_g = G_A * QPAD

    def update(kg, vg, bias):
        bias = jnp.concatenate([bias] * (KV_A * G_A), axis=0)
        sc = jnp.concatenate(
            [lax.dot_general(q_ref[g], kg[g], nt, preferred_element_type=jnp.float32) for g in range(KV_A)],
            axis=0) * scale + bias
        m_old = m_scr[...]
        m_new = jnp.maximum(m_old, jnp.max(sc, axis=1, keepdims=True))
        alpha = jnp.exp(m_old - m_new)
        p = jnp.exp(sc - m_new)
        l_scr[...] = alpha * l_scr[...] + jnp.sum(p, axis=1, keepdims=True)
        pb = p.astype(jnp.bfloat16)
        pv = jnp.concatenate(
            [jnp.dot(pb[g * rows_g:(g + 1) * rows_g], vg[g], preferred_element_type=jnp.float32)
             for g in range(KV_A)], axis=0)
        acc_scr[...] = alpha * acc_scr[...] + pv
        m_scr[...] = m_new

    def head_rows(page_refs, g):
        return jnp.concatenate([r[pl.ds(g, PAGE_SIZE, stride=KV_A), :] for r in page_refs],
                               axis=0).astype(jnp.bfloat16)

    update([head_rows(k_refs, g) for g in range(KV_A)], [head_rows(v_refs, g) for g in range(KV_A)],
           bias_ref[...])

    @pl.when(s == ns - 1)
    def _():
        kn, vn = kn_ref[...], vn_ref[...]
        update([kn[:, g * HD_A:(g + 1) * HD_A] for g in range(KV_A)],
               [vn[:, g * HD_A:(g + 1) * HD_A] for g in range(KV_A)], bias_new_ref[...])
        o_ref[...] = acc_scr[...] / l_scr[...]


def _sample_sparse_attn(q, qi, wi, ki, k, v, cache_k, cache_v, cache_kidx, page_table, *, layer, n_sel):
    db, s_new = q.shape[:2]
    n_pages = page_table.shape[1]
    ppi, ppa = min(PAGES_PER_INDEX_STEP, n_pages), min(PAGES_PER_STEP, n_pages)
    nsi, nsa = n_pages // ppi, n_pages // ppa
    kvw = KV_A * HD_A
    rows = KV_A * G_A * QPAD
    pad_q = ((0, 0), (0, QPAD - s_new), (0, 0), (0, 0))
    pad_k = ((0, 0), (0, PAGE_SIZE - s_new), (0, 0))
    qi_r = jnp.pad(qi.reshape(db, s_new, H_IDX, HD_IDX), pad_q).reshape(db, QPAD * H_IDX, HD_IDX)
    w_r = jnp.pad(wi, ((0, 0), (0, QPAD - s_new), (0, 0))).reshape(db, QPAD * H_IDX, 1)
    ki_n = jnp.swapaxes(jnp.pad(ki.astype(jnp.bfloat16), pad_k), 1, 2)
    k_n = jnp.pad(k, pad_k)
    v_n = jnp.pad(v, pad_k)
    q_r = jnp.pad(q.reshape(db, s_new, KV_A, G_A, HD_A).transpose(0, 2, 3, 1, 4),
                  ((0, 0), (0, 0), (0, 0), (0, QPAD - s_new), (0, 0))).reshape(db, KV_A, G_A * QPAD, HD_A)

    def page_spec(rows_, pp, j):
        return pl.BlockSpec((None, None, rows_, LANES), lambda b, s, pt: (layer, pt[b, s * pp + j], 0, 0))

    cki = jnp.swapaxes(cache_kidx, 2, 3)
    ck = cache_k.reshape(cache_k.shape[:2] + (PAGE_SIZE * KV_A, HD_A))
    cv = cache_v.reshape(cache_v.shape[:2] + (PAGE_SIZE * KV_A, HD_A))

    bias, bias_new = pl.pallas_call(
        functools.partial(_ssi_kernel, pp=ppi, n_sel=n_sel, s_new=s_new, ns=nsi),
        grid_spec=pltpu.PrefetchScalarGridSpec(
            num_scalar_prefetch=1,
            grid=(db, nsi),
            in_specs=[
                pl.BlockSpec((None, QPAD * H_IDX, HD_IDX), lambda b, s, pt: (b, 0, 0)),
                pl.BlockSpec((None, QPAD * H_IDX, 1), lambda b, s, pt: (b, 0, 0)),
                pl.BlockSpec((None, HD_IDX, PAGE_SIZE), lambda b, s, pt: (b, 0, 0)),
            ] + [page_spec(HD_IDX, ppi, j) for j in range(ppi)],
            out_specs=[
                pl.BlockSpec((None, QPAD, n_pages * PAGE_SIZE), lambda b, s, pt: (b, 0, 0)),
                pl.BlockSpec((None, QPAD, PAGE_SIZE), lambda b, s, pt: (b, 0, 0)),
            ],
            scratch_shapes=[pltpu.VMEM((nsi, QPAD, ppi * PAGE_SIZE), jnp.int32)],
        ),
        out_shape=[jax.ShapeDtypeStruct((db, QPAD, n_pages * PAGE_SIZE), jnp.float32),
                   jax.ShapeDtypeStruct((db, QPAD, PAGE_SIZE), jnp.float32)],
        compiler_params=_params("parallel", "arbitrary"),
        name="sample_indexer",
    )(page_table, qi_r, w_r, ki_n, *([cki] * ppi))

    o = pl.pallas_call(
        functools.partial(_ssa_kernel, pp=ppa, ns=nsa),
        grid_spec=pltpu.PrefetchScalarGridSpec(
            num_scalar_prefetch=1,
            grid=(db, nsa),
            in_specs=[
                pl.BlockSpec((None, KV_A, G_A * QPAD, HD_A), lambda b, s, pt: (b, 0, 0, 0)),
                pl.BlockSpec((None, QPAD, ppa * PAGE_SIZE), lambda b, s, pt: (b, 0, s)),
                pl.BlockSpec((None, QPAD, PAGE_SIZE), lambda b, s, pt: (b, 0, 0)),
                pl.BlockSpec((None, PAGE_SIZE, kvw), lambda b, s, pt: (b, 0, 0)),
                pl.BlockSpec((None, PAGE_SIZE, kvw), lambda b, s, pt: (b, 0, 0)),
            ] + [page_spec(ck.shape[2], ppa, j) for j in range(ppa)] * 2,
            out_specs=pl.BlockSpec((None, rows, HD_A), lambda b, s, pt: (b, 0, 0)),
            scratch_shapes=[
                pltpu.VMEM((rows, 1), jnp.float32),
                pltpu.VMEM((rows, 1), jnp.float32),
                pltpu.VMEM((rows, HD_A), jnp.float32),
            ],
        ),
        out_shape=jax.ShapeDtypeStruct((db, rows, HD_A), jnp.float32),
        compiler_params=_params("parallel", "arbitrary"),
        name="sample_sparse_attn",
    )(page_table, q_r, bias, bias_new, k_n, v_n, *([ck] * ppa), *([cv] * ppa))
    o = o.reshape(db, KV_A, G_A, QPAD, HD_A)[:, :, :, :s_new].transpose(0, 3, 1, 2, 4)
    return o.reshape(db * s_new, H_A * HD_A).astype(jnp.bfloat16)


def _wa_kernel(q_ref, kp_ref, kc_ref, vp_ref, vc_ref, sink_ref, o_ref, *, tq, blocks_per_seq, mask_first):
    n = pl.program_id(0)
    nt = (((1,), (1,)), ((), ()))
    r = lax.broadcasted_iota(jnp.int32, (tq, 2 * WINDOW), 0)
    c = lax.broadcasted_iota(jnp.int32, (tq, 2 * WINDOW), 1)
    ok = (c - r >= 1) & (c - r <= WINDOW)
    if mask_first:
        ok = ok & ((n % blocks_per_seq != 0) | (c >= WINDOW))
    bias = jnp.where(ok, 0.0, NEG)
    bias = jnp.concatenate([bias] * G_B, axis=0)
    scale = HD_B ** -0.5
    for g in range(KV_B):
        cols = slice(g * HD_B, (g + 1) * HD_B)
        kcat = jnp.concatenate([kp_ref[:, cols], kc_ref[:, cols]], axis=0)
        vcat = jnp.concatenate([vp_ref[:, cols], vc_ref[:, cols]], axis=0)
        qg = jnp.concatenate(
            [q_ref[:, (g * G_B + j) * HD_B:(g * G_B + j + 1) * HD_B] for j in range(G_B)], axis=0)
        s = lax.dot_general(qg, kcat, nt, preferred_element_type=jnp.float32) * scale + bias
        sink = sink_ref[g]
        m = jnp.maximum(jnp.max(s, axis=1, keepdims=True), sink)
        p = jnp.exp(s - m)
        p = p / (jnp.sum(p, axis=1, keepdims=True) + jnp.exp(sink - m))
        og = jnp.dot(p.astype(jnp.bfloat16), vcat, preferred_element_type=jnp.float32)
        for j in range(G_B):
            h = g * G_B + j
            o_ref[:, h * HD_B:(h + 1) * HD_B] = og[j * tq:(j + 1) * tq].astype(o_ref.dtype)


def _wat_kernel(q_ref, kp_ref, kc_ref, vtp_ref, vtc_ref, sink_ref, o_ref, *, blocks_per_seq):
    tq = WINDOW
    n = pl.program_id(0)
    nt = (((1,), (1,)), ((), ()))
    kidx = lax.broadcasted_iota(jnp.int32, (2 * WINDOW, tq), 0)
    qidx = lax.broadcasted_iota(jnp.int32, (2 * WINDOW, tq), 1)
    ok = (kidx - qidx >= 1) & (kidx - qidx <= WINDOW) & ((n % blocks_per_seq != 0) | (kidx >= WINDOW))
    bias = jnp.where(ok, 0.0, NEG)
    bias = jnp.concatenate([bias] * G_B, axis=1)
    scale = HD_B ** -0.5
    for g in range(KV_B):
        cols = slice(g * HD_B, (g + 1) * HD_B)
        kcat = jnp.concatenate([kp_ref[:, cols], kc_ref[:, cols]], axis=0)
        vcat_t = jnp.concatenate([vtp_ref[cols, :], vtc_ref[cols, :]], axis=1)
        qg = jnp.concatenate(
            [q_ref[:, (g * G_B + j) * HD_B:(g * G_B + j + 1) * HD_B] for j in range(G_B)], axis=0)
        s = lax.dot_general(kcat, qg, nt, preferred_element_type=jnp.float32) * scale + bias
        sink = sink_ref[g]
        m = jnp.maximum(jnp.max(s, axis=0, keepdims=True), sink)
        p = jnp.exp(s - m)
        inv = 1.0 / (jnp.sum(p, axis=0, keepdims=True) + jnp.exp(sink - m))
        ot = jnp.dot(vcat_t, (p * inv).astype(jnp.bfloat16), preferred_element_type=jnp.float32)
        for j in range(0, G_B, 2):
            pair = jnp.concatenate([ot[:, j * tq:(j + 1) * tq], ot[:, (j + 1) * tq:(j + 2) * tq]], axis=0)
            h = g * G_B + j
            o_ref[:, h * HD_B:(h + 2) * HD_B] = pair.T.astype(o_ref.dtype)


def _window_attn_t(q, k, v_t, sinks, *, blocks_per_seq):
    tq = WINDOW
    nblk = q.shape[0] // tq
    kvw = KV_B * HD_B
    sink_rows = jnp.repeat(sinks.astype(jnp.float32).reshape(KV_B, G_B), tq, axis=1)[:, None, :]
    return pl.pallas_call(
        functools.partial(_wat_kernel, blocks_per_seq=blocks_per_seq),
        grid=(nblk,),
        in_specs=[
            pl.BlockSpec((tq, q.shape[1]), lambda n: (n, 0)),
            pl.BlockSpec((WINDOW, kvw), lambda n: (jnp.maximum(n - 1, 0), 0)),
            pl.BlockSpec((WINDOW, kvw), lambda n: (n, 0)),
            pl.BlockSpec((kvw, WINDOW), lambda n: (0, jnp.maximum(n - 1, 0))),
            pl.BlockSpec((kvw, WINDOW), lambda n: (0, n)),
            pl.BlockSpec((KV_B, 1, G_B * tq), lambda n: (0, 0, 0)),
        ],
        out_specs=pl.BlockSpec((tq, q.shape[1]), lambda n: (n, 0)),
        out_shape=jax.ShapeDtypeStruct(q.shape, jnp.bfloat16),
        compiler_params=_params("parallel"),
        name="window_attn_t",
    )(q, k, k, v_t, v_t, sink_rows)


def _window_attn(q, k_prev, k_cur, v_prev, v_cur, sinks, *, tq, blocks_per_seq, mask_first):
    nblk = q.shape[0] // tq
    kvw = KV_B * HD_B
    if mask_first:
        prev = lambda n: (jnp.maximum(n - 1, 0), 0)
    else:
        prev = lambda n: (n, 0)
    cur = lambda n: (n, 0)
    sink_rows = jnp.repeat(sinks.astype(jnp.float32).reshape(KV_B, G_B), tq, axis=1)[:, :, None]
    return pl.pallas_call(
        functools.partial(_wa_kernel, tq=tq, blocks_per_seq=blocks_per_seq, mask_first=mask_first),
        grid=(nblk,),
        in_specs=[
            pl.BlockSpec((tq, q.shape[1]), cur),
            pl.BlockSpec((WINDOW, kvw), prev),
            pl.BlockSpec((WINDOW, kvw), cur),
            pl.BlockSpec((WINDOW, kvw), prev),
            pl.BlockSpec((WINDOW, kvw), cur),
            pl.BlockSpec((KV_B, G_B * tq, 1), lambda n: (0, 0, 0)),
        ],
        out_specs=pl.BlockSpec((tq, q.shape[1]), cur),
        out_shape=jax.ShapeDtypeStruct(q.shape, jnp.bfloat16),
        compiler_params=_params("parallel"),
        name="window_attn",
    )(q, k_prev, k_cur, v_prev, v_cur, sink_rows)


def _trunk(x, c_mod, kv_mod, f_mod, pos, weights, *, tm, tm_mm, sparse_attn, make_ctx):
    (g_norm, w_in_a, w_out_a, g_kv, w_kv_b, w_q_b, sinks_b, w_out_b, w_gate_d, w_up_d, w_down_d,
     w_router, b_router, w_gate_e, w_up_e, w_down_e, g_final) = weights
    b, t, d = x.shape
    m = b * t
    per_row = tm > t
    tps = 1 if per_row else t // tm
    tps_mm = 1 if per_row else t // tm_mm

    def seq(v):
        if per_row:
            return jnp.repeat(v, t, axis=0)[None]
        return v[:, None, :]

    pos_rows = jnp.tile(pos, b) if per_row else pos
    tab_a = _rope_table(pos_rows, HD_A, ROT_A)
    tab_i = _rope_table(pos_rows, HD_IDX, ROT_IDX)
    tab_t = _tail_table(pos_rows)
    tab_b = _rope_table(pos_rows, HD_B, ROT_B)

    xf = x.reshape(m, d)
    rows_a = []
    ctx = None
    for l in range(DEPTH):
        if l == N_A_LAYERS:
            shift, scale = jnp.split(kv_mod, 2, axis=-1)
            h = _adaln(xf, g_kv, seq(shift), seq(scale), tm=tm, tiles_per_seq=tps)
            kvw_b = KV_B * HD_B
            kb, kb_bf = _matmul(h, w_kv_b, col0=0, ncols=kvw_b, tm=tm_mm, tn=kvw_b, rope=(tab_b, ROT_B // 2),
                                out_dtypes=(jnp.float32, jnp.bfloat16))
            vb, vb_bf = _matmul(h, w_kv_b, col0=kvw_b, ncols=kvw_b, tm=tm_mm, tn=kvw_b,
                                out_dtypes=(jnp.float32, jnp.bfloat16))
            ctx = make_ctx(kb.reshape(b, t, KV_B, HD_B), vb.reshape(b, t, KV_B, HD_B), kb_bf, vb_bf)
        sh1, sc1, g1, sh2, sc2, g2 = jnp.split(c_mod[l], 6, axis=-1)
        h = _adaln(xf, g_norm[l, 0], seq(sh1), seq(sc1), tm=tm, tiles_per_seq=tps)
        if l < N_A_LAYERS:
            both = (jnp.float32, jnp.bfloat16)
            only_bf = (jnp.bfloat16,)
            kvw = KV_A * HD_A
            q = _matmul(h, w_in_a, layer=l, col0=0, ncols=QW_A, tm=tm_mm, tn=512, rope=(tab_a, ROT_A // 2),
                        out_dtypes=only_bf)
            k, k_bf = _matmul(h, w_in_a, layer=l, col0=QW_A, ncols=kvw, tm=tm_mm, tn=kvw,
                              rope=(tab_a, ROT_A // 2), out_dtypes=both)
            v, v_bf = _matmul(h, w_in_a, layer=l, col0=QW_A + kvw, ncols=kvw, tm=tm_mm, tn=kvw, out_dtypes=both)
            qi = _matmul(h, w_in_a, layer=l, col0=QW_A + 2 * kvw, ncols=H_IDX * HD_IDX, tm=tm_mm, tn=512,
                         rope=(tab_i, ROT_IDX // 2), out_dtypes=only_bf)
            w_tail = jnp.pad(w_in_a[l][:, QW_A + 2 * kvw + H_IDX * HD_IDX:],
                             ((0, 0), (0, LANES - HD_IDX - H_IDX)))
            tail = _matmul(h, w_tail, tm=tm_mm, tn=LANES, rope=(tab_t, ROT_IDX // 2))
            o = sparse_attn(l, q, qi, k_bf, v_bf, tail)
            rows_a.append((k.reshape(b, t, KV_A, HD_A), v.reshape(b, t, KV_A, HD_A),
                           tail[:, :HD_IDX].reshape(b, t, HD_IDX)))
            w_o, lo = w_out_a, l
        else:
            lb = l - N_A_LAYERS
            q = _matmul(h, w_q_b, layer=lb, tm=tm_mm, tn=512, rope=(tab_b, ROT_B // 2),
                        out_dtypes=(jnp.bfloat16,))
            o = ctx[0](q, sinks_b[lb])
            w_o, lo = w_out_b, lb
        xf = _matmul(o, w_o, layer=lo, tm=tm_mm, tn=512, res=(xf, seq(g1), tps_mm))
        if l % 2 == 0:
            h = _adaln(xf, g_norm[l, 1], seq(sh2), seq(sc2), tm=tm, tiles_per_seq=tps)
            xf = _ffn(h, w_gate_d, w_up_d, w_down_d, layer=l // 2, tm=tm, tf=256, res=(xf, seq(g2), tps))
        else:
            h, logits = _adaln(xf, g_norm[l, 1], seq(sh2), seq(sc2), tm=tm, tiles_per_seq=tps,
                               out_dtype=jnp.float32, router=(w_router[l // 2], b_router[l // 2]))
            tmc = min(tm, 256)
            xf = yield (l // 2, h, logits, (xf, seq(g2), tmc, 1 if per_row else t // tmc))
    fsh, fsc = jnp.split(f_mod, 2, axis=-1)
    y = _adaln(xf, g_final, seq(fsh), seq(fsc), tm=tm, tiles_per_seq=tps, out_dtype=jnp.float32)
    return y.reshape(b, t, d), rows_a, ctx


def kernel(x_prompt, x_sample, cache_k_a, cache_v_a, cache_kidx_a, state_k_b, state_v_b, page_table,
           c_prompt, c_sample, g_norm, w_mod, b_mod, w_in_a, w_out_a, g_kv, w_kvmod, b_kvmod, w_kv_b,
           w_q_b, sinks_b, w_out_b, w_gate_d, w_up_d, w_down_d, w_router, b_router, w_gate_e, w_up_e,
           w_down_e, g_final, w_fmod, b_fmod):
    n_pages = page_table.shape[1]
    past = n_pages * PAGE_SIZE
    nb_p, nb_s = c_prompt.shape[0], c_sample.shape[0]

    c_rows = 48
    c_all = jnp.zeros((c_rows, D_MODEL), jnp.bfloat16)
    c_all = c_all.at[:nb_p + nb_s].set(jnp.concatenate([c_prompt, c_sample], axis=0).astype(jnp.bfloat16))
    mods = [_matmul(c_all, w_mod, layer=l, tm=c_rows, tn=512, bias=b_mod[l][None]) for l in range(DEPTH)]
    kvm = _matmul(c_all, w_kvmod, tm=c_rows, tn=512, bias=b_kvmod[None])
    fm = _matmul(c_all, w_fmod, tm=c_rows, tn=512, bias=b_fmod[None])

    sb, st = x_sample.shape[:2]
    w_buf = state_k_b.shape[1]
    kvw_b = KV_B * HD_B

    def prompt_sparse_attn(li, q, qi, k_bf, v_bf, tail):
        b, t = x_prompt.shape[:2]
        return _prompt_sparse_attn_t(q, qi, k_bf, v_bf, tail, b=b, t=t, tq=256, tk=256,
                                   n_sel=min(TOPK_MAX, t // 4))

    def sample_sparse_attn(li, q, qi, k_bf, v_bf, tail):
        r3 = lambda a: a.reshape(sb, st, a.shape[-1])
        return _sample_sparse_attn(
            r3(q), r3(qi), r3(tail[:, HD_IDX:HD_IDX + H_IDX]), r3(tail[:, :HD_IDX]), r3(k_bf), r3(v_bf),
            cache_k_a, cache_v_a, cache_kidx_a, page_table, layer=li, n_sel=min(TOPK_MAX, (past + st) // 4))

    def prompt_ctx(kb, vb, kb_bf, vb_bf):
        t = kb.shape[1]
        wp = min(WINDOW, t)

        vb_t = vb_bf.T

        def attend(q, sinks):
            return _window_attn_t(q, kb_bf, vb_t, sinks, blocks_per_seq=t // WINDOW)

        return attend, kb[:, t - wp:], vb[:, t - wp:]

    def sample_ctx(kb, vb, kb_bf, vb_bf):
        tq = 16
        pad_new = lambda a: jnp.pad(a.reshape(sb, st, kvw_b), ((0, 0), (0, WINDOW - st), (0, 0))).reshape(-1, kvw_b)
        k_prev = state_k_b.astype(jnp.bfloat16).reshape(sb * w_buf, kvw_b)
        v_prev = state_v_b.astype(jnp.bfloat16).reshape(sb * w_buf, kvw_b)
        k_cur, v_cur = pad_new(kb_bf), pad_new(vb_bf)

        def attend(q, sinks):
            qp = jnp.pad(q.reshape(sb, st, -1), ((0, 0), (0, tq - st), (0, 0))).reshape(sb * tq, -1)
            o = _window_attn(qp, k_prev, k_cur, v_prev, v_cur, sinks, tq=tq, blocks_per_seq=1, mask_first=False)
            return o.reshape(sb, tq, -1)[:, :st].reshape(sb * st, -1)

        k_ctx = jnp.concatenate([state_k_b, kb], axis=1)
        v_ctx = jnp.concatenate([state_v_b, vb], axis=1)
        return attend, k_ctx[:, -w_buf:], v_ctx[:, -w_buf:]

    weights = (g_norm, w_in_a, w_out_a, g_kv, w_kv_b, w_q_b, sinks_b, w_out_b, w_gate_d, w_up_d, w_down_d,
               w_router, b_router, w_gate_e, w_up_e, w_down_e, g_final)
    pos_p = jnp.arange(x_prompt.shape[1])
    pos_s = past + jnp.arange(x_sample.shape[1])
    trunks = [
        _trunk(x_prompt, [mm[:nb_p] for mm in mods], kvm[:nb_p], fm[:nb_p], pos_p, weights,
               tm=1024, tm_mm=2048, sparse_attn=prompt_sparse_attn, make_ctx=prompt_ctx),
        _trunk(x_sample, [mm[nb_p:nb_p + nb_s] for mm in mods], kvm[nb_p:nb_p + nb_s], fm[nb_p:nb_p + nb_s], pos_s,
               weights, tm=sb * st, tm_mm=sb * st, sparse_attn=sample_sparse_attn, make_ctx=sample_ctx),
    ]
    requests = [next(tr) for tr in trunks]
    results = [None] * len(trunks)
    while any(r is None for r in results):
        layer = requests[0][0]
        new_x = _moe([r[1] for r in requests], [r[2] for r in requests], w_gate_e, w_up_e, w_down_e,
                     layer=layer, tm=MOE_ROW_TILE, tf=256, res_list=[r[3] for r in requests])
        for n, (tr, xn) in enumerate(zip(trunks, new_x)):
            try:
                requests[n] = tr.send(xn)
            except StopIteration as done:
                results[n] = done.value
    (y_prompt, rows_p, ctx_p), (y_sample, rows_s, ctx_s) = results

    return (y_prompt, y_sample,
            jnp.stack([r[0] for r in rows_p]), jnp.stack([r[1] for r in rows_p]), jnp.stack([r[2] for r in rows_p]),
            jnp.stack([r[0] for r in rows_s]), jnp.stack([r[1] for r in rows_s]), jnp.stack([r[2] for r in rows_s]),
            ctx_p[1], ctx_p[2], ctx_s[1], ctx_s[2])
```

```python
import functools
import math

import jax
import jax.numpy as jnp
import numpy as np
from jax import lax
from jax.experimental import pallas as pl
from jax.experimental.pallas import tpu as pltpu

D_MODEL = 2048
DEPTH = 4
PAGE_SIZE = 128
HD_A = 128
H_A = D_MODEL // HD_A
KV_A = 4
G_A = H_A // KV_A
QW_A = H_A * HD_A
H_IDX = 16
HD_IDX = 64
TOPK_MAX = 256
IDX_W_SCALE = (H_IDX * HD_IDX) ** -0.5
Q_BLOCK = 128
HD_B = 64
H_B = D_MODEL // HD_B
KV_B = 4
G_B = H_B // KV_B
QW_B = H_B * HD_B
KVW_B = 2 * KV_B * HD_B
WINDOW = 128
ROPE_THETA = 500000.0
ROT_A = HD_A // 4
ROT_IDX = HD_IDX // 4
ROT_B = HD_B // 4
N_EXPERTS = 8
TOP_K_EXPERTS = 2
EPS = 1e-6
N_A_LAYERS = DEPTH // 2

LANES = 128
VMEM_LIMIT = 56 * 1024 * 1024
MOE_ROW_TILE = 2304


def _params(*sem):
    return pltpu.CompilerParams(dimension_semantics=sem, vmem_limit_bytes=VMEM_LIMIT)


def _rope_table(pos, head_dim, rot):
    half = rot // 2
    inv = jnp.exp(-math.log(ROPE_THETA) * jnp.arange(half, dtype=jnp.float32) / half)
    ang = pos.astype(jnp.float32)[:, None] * inv[None, :]
    cos, sin = jnp.cos(ang), jnp.sin(ang)
    d = np.arange(LANES) % head_dim
    idx = np.where(d < half, d, np.clip(d - half, 0, half - 1))
    cos_l, sin_l = cos[:, idx], sin[:, idx]
    c = jnp.where((d < rot)[None], cos_l, 1.0)
    u = jnp.where(((d >= half) & (d < rot))[None], sin_l, 0.0)
    dn = jnp.where((d < half)[None], -sin_l, 0.0)
    return jnp.concatenate([c, u, dn], axis=1)


def _tail_table(pos):
    t = _rope_table(pos, HD_IDX, ROT_IDX)
    lane = np.arange(LANES)
    c = jnp.where((lane < HD_IDX)[None], t[:, :LANES],
                  jnp.where((lane < HD_IDX + H_IDX)[None], IDX_W_SCALE, 1.0))
    keep = (lane < HD_IDX)[None]
    u = jnp.where(keep, t[:, LANES:2 * LANES], 0.0)
    dn = jnp.where(keep, t[:, 2 * LANES:], 0.0)
    return jnp.concatenate([c, u, dn], axis=1)


def _adaln_kernel(*refs, has_router):
    x_ref, g_ref, sh_ref, sc_ref = refs[:4]
    x = x_ref[...]
    y = x * lax.rsqrt(jnp.mean(x * x, axis=-1, keepdims=True) + EPS)
    h = (y * g_ref[...]) * (1.0 + sc_ref[0]) + sh_ref[0]
    if has_router:
        wr_ref, br_ref, h_ref, lg_ref = refs[4:]
        h_ref[...] = h.astype(h_ref.dtype)
        lg_ref[...] = jnp.dot(h.astype(jnp.bfloat16), wr_ref[...], preferred_element_type=jnp.float32) + br_ref[...]
    else:
        h_ref = refs[4]
        h_ref[...] = h.astype(h_ref.dtype)


def _adaln(x, g, shift, scale, *, tm, tiles_per_seq, out_dtype=jnp.bfloat16, router=None):
    m, d = x.shape
    r = shift.shape[1]
    in_specs = [
        pl.BlockSpec((tm, d), lambda i: (i, 0)),
        pl.BlockSpec((1, d), lambda i: (0, 0)),
        pl.BlockSpec((1, r, d), lambda i: (i // tiles_per_seq, 0, 0)),
        pl.BlockSpec((1, r, d), lambda i: (i // tiles_per_seq, 0, 0)),
    ]
    args = [x, g.reshape(1, d), shift, scale]
    out_shape = [jax.ShapeDtypeStruct((m, d), out_dtype)]
    out_specs = [pl.BlockSpec((tm, d), lambda i: (i, 0))]
    if router is not None:
        w_r, b_r = router
        wr = jnp.zeros((d, LANES), jnp.bfloat16).at[:, :N_EXPERTS].set(w_r.astype(jnp.bfloat16))
        br = jnp.zeros((1, LANES), jnp.float32).at[0, :N_EXPERTS].set(b_r.astype(jnp.float32))
        in_specs += [pl.BlockSpec((d, LANES), lambda i: (0, 0)), pl.BlockSpec((1, LANES), lambda i: (0, 0))]
        args += [wr, br]
        out_shape.append(jax.ShapeDtypeStruct((m, LANES), jnp.float32))
        out_specs.append(pl.BlockSpec((tm, LANES), lambda i: (i, 0)))
    outs = pl.pallas_call(
        functools.partial(_adaln_kernel, has_router=router is not None),
        grid=(m // tm,),
        in_specs=in_specs,
        out_specs=out_specs,
        out_shape=out_shape,
        compiler_params=_params("parallel"),
        name="adaln",
    )(*args)
    return outs if router is not None else outs[0]


def _mm_kernel(*refs, tn, has_bias, rope_half, has_res):
    it = iter(refs)
    a_ref = next(it)
    w_ref = next(it)
    bias_ref = next(it) if has_bias else None
    tab_ref = next(it) if rope_half else None
    res_ref = next(it) if has_res else None
    gate_ref = next(it) if has_res else None
    out_refs = list(it)
    acc = jnp.dot(a_ref[...], w_ref[...].astype(jnp.bfloat16), preferred_element_type=jnp.float32)
    if has_bias:
        acc = acc + bias_ref[...]
    if rope_half:
        c = tab_ref[:, :LANES]
        u = tab_ref[:, LANES:2 * LANES]
        dn = tab_ref[:, 2 * LANES:]
        parts = []
        for grp in range(tn // LANES):
            xg = acc[:, grp * LANES:(grp + 1) * LANES]
            parts.append(xg * c + pltpu.roll(xg, rope_half, 1) * u + pltpu.roll(xg, LANES - rope_half, 1) * dn)
        acc = parts[0] if len(parts) == 1 else jnp.concatenate(parts, axis=1)
    if has_res:
        acc = res_ref[...] + gate_ref[0] * acc
    for o in out_refs:
        o[...] = acc.astype(o.dtype)


def _matmul(a, w, *, layer=0, col0=0, ncols=None, tm, tn, bias=None, rope=None, res=None,
            out_dtypes=(jnp.float32,)):
    if w.ndim == 2:
        w = w[None]
    m, k = a.shape
    ncols = w.shape[2] - col0 if ncols is None else ncols
    assert ncols % tn == 0 and col0 % tn == 0 and m % tm == 0
    cb = col0 // tn
    in_specs = [
        pl.BlockSpec((tm, k), lambda i, j: (i, 0)),
        pl.BlockSpec((None, k, tn), lambda i, j: (layer, 0, cb + j)),
    ]
    args = [a, w]
    if bias is not None:
        in_specs.append(pl.BlockSpec((1, tn), lambda i, j: (0, cb + j)))
        args.append(bias)
    rope_half = 0
    if rope is not None:
        tab, rope_half = rope
        tab_tiles = tab.shape[0] // tm
        in_specs.append(pl.BlockSpec((tm, 3 * LANES), lambda i, j: (i % tab_tiles, 0)))
        args.append(tab)
    if res is not None:
        resid, gate, tps = res
        r = gate.shape[1]
        in_specs.append(pl.BlockSpec((tm, tn), lambda i, j: (i, j)))
        in_specs.append(pl.BlockSpec((1, r, tn), lambda i, j: (i // tps, 0, j)))
        args += [resid, gate]
    outs = pl.pallas_call(
        functools.partial(_mm_kernel, tn=tn, has_bias=bias is not None, rope_half=rope_half,
                          has_res=res is not None),
        grid=(m // tm, ncols // tn),
        in_specs=in_specs,
        out_specs=[pl.BlockSpec((tm, tn), lambda i, j: (i, j)) for _ in out_dtypes],
        out_shape=[jax.ShapeDtypeStruct((m, ncols), dt) for dt in out_dtypes],
        compiler_params=_params("parallel", "arbitrary"),
        name="matmul",
    )(*args)
    return outs if len(outs) > 1 else outs[0]


def _start_row_gather(idx_ref, base, src_hbm, dst, sem, n_rows):
    def issue(r, carry):
        pltpu.make_async_copy(src_hbm.at[pl.ds(idx_ref[base + r], 1)], dst.at[pl.ds(r, 1)], sem).start()
        return carry

    lax.fori_loop(0, n_rows, issue, 0, unroll=8)


def _wait_row_gather(src_hbm, dst, sem, n_rows):
    pltpu.make_async_copy(src_hbm.at[pl.ds(0, n_rows)], dst, sem).wait()


def _ffn_kernel(te_ref, tv_ref, rt_ref, *refs, nj, mode, tm):
    del te_ref
    if mode == "dense":
        x_ref, wg_ref, wu_ref, wd_ref, res_ref, gate_ref, o_ref = refs
    else:
        x_hbm, wg_ref, wu_ref, wd_ref, rs_ref, o_hbm, o_ref, x_ref, sem, sem_out = refs
    i = pl.program_id(0)
    j = pl.program_id(1)

    if mode == "routed":
        @pl.when((j == 0) & (tv_ref[i] > 0))
        def _():
            _start_row_gather(rt_ref, i * tm, x_hbm, o_ref, sem, tm)
            _wait_row_gather(x_hbm, o_ref, sem, tm)
            x_ref[...] = o_ref[...].astype(x_ref.dtype)

    @pl.when(j == 0)
    def _():
        o_ref[...] = jnp.zeros_like(o_ref)

    @pl.when(tv_ref[i] > 0)
    def _():
        x = x_ref[...]
        g = jnp.dot(x, wg_ref[...].astype(jnp.bfloat16), preferred_element_type=jnp.float32)
        u = jnp.dot(x, wu_ref[...].astype(jnp.bfloat16), preferred_element_type=jnp.float32)
        act = ((g * (1.0 / (1.0 + jnp.exp(-g)))) * u).astype(jnp.bfloat16)
        d = o_ref.shape[1]
        cw = min(d, 512)
        for c0 in range(0, d, cw):
            o_ref[:, c0:c0 + cw] += jnp.dot(act, wd_ref[:, c0:c0 + cw].astype(jnp.bfloat16),
                                            preferred_element_type=jnp.float32)

    @pl.when(j == nj - 1)
    def _():
        if mode == "dense":
            o_ref[...] = res_ref[...] + gate_ref[0] * o_ref[...]
        else:
            o_ref[...] = rs_ref[...] * o_ref[...]
            out_copy = pltpu.make_async_copy(o_ref, o_hbm.at[pl.ds(i * tm, tm)], sem_out)
            out_copy.start()
            out_copy.wait()


def _ffn(x, wg, wu, wd, *, layer, tm, tf, tile_expert=None, tile_valid=None, row_token=None, res=None,
         row_scale=None):
    d = x.shape[1]
    f = wg.shape[-1]
    dense = res is not None
    m = x.shape[0] if dense else row_token.shape[0]
    nt, nj = m // tm, f // tf
    if dense:
        tile_expert = jnp.full((nt,), layer, jnp.int32)
        tile_valid = jnp.ones((nt,), jnp.int32)
        row_token = jnp.zeros((1,), jnp.int32)
    else:
        tile_expert = tile_expert + layer * wg.shape[1]
        wg, wu, wd = wg.reshape(-1, d, f), wu.reshape(-1, d, f), wd.reshape(-1, f, d)

    def jj(i, j, tv):
        return jnp.where(tv[i] > 0, j, nj - 1)

    in_specs = [
        pl.BlockSpec((tm, d), lambda i, j, te, tv, rt: (i, 0)) if dense else pl.BlockSpec(memory_space=pl.ANY),
        pl.BlockSpec((None, d, tf), lambda i, j, te, tv, rt: (te[i], 0, jj(i, j, tv))),
        pl.BlockSpec((None, d, tf), lambda i, j, te, tv, rt: (te[i], 0, jj(i, j, tv))),
        pl.BlockSpec((None, tf, d), lambda i, j, te, tv, rt: (te[i], jj(i, j, tv), 0)),
    ]
    args = [x, wg, wu, wd]
    scratch = []
    if dense:
        resid, gate, tps = res
        r = gate.shape[1]
        in_specs.append(pl.BlockSpec((tm, d), lambda i, j, te, tv, rt: (i, 0), pipeline_mode=pl.Buffered(1)))
        in_specs.append(pl.BlockSpec((1, r, d), lambda i, j, te, tv, rt: (i // tps, 0, 0)))
        args += [resid, gate]
    else:
        in_specs.append(pl.BlockSpec((tm, 1), lambda i, j, te, tv, rt: (i, 0)))
        args.append(row_scale)
        scratch = [pltpu.VMEM((tm, d), jnp.float32), pltpu.VMEM((tm, d), jnp.bfloat16),
                   pltpu.SemaphoreType.DMA(()), pltpu.SemaphoreType.DMA(())]
    out_spec = (pl.BlockSpec((tm, d), lambda i, j, te, tv, rt: (i, 0)) if dense
                else pl.BlockSpec(memory_space=pl.ANY))
    return pl.pallas_call(
        functools.partial(_ffn_kernel, nj=nj, mode="dense" if dense else "routed", tm=tm),
        grid_spec=pltpu.PrefetchScalarGridSpec(
            num_scalar_prefetch=3,
            grid=(nt, nj),
            in_specs=in_specs,
            out_specs=out_spec,
            scratch_shapes=scratch,
        ),
        out_shape=jax.ShapeDtypeStruct((m, d), jnp.float32),
        compiler_params=_params("parallel" if dense else "arbitrary", "arbitrary"),
        name="ffn_dense" if dense else "ffn_routed",
    )(tile_expert, tile_valid, row_token, *args)


def _combine_kernel(dest_ref, ys_hbm, res_ref, gate_ref, o_ref, buf, sem, *, tm):
    i = pl.program_id(0)
    npair_half = pl.num_programs(0) * tm

    def issue(r, carry):
        for k in range(TOP_K_EXPERTS):
            row = dest_ref[k * npair_half + i * tm + r]
            pltpu.make_async_copy(ys_hbm.at[pl.ds(row, 1)], buf.at[pl.ds(k * tm + r, 1)], sem).start()
        return carry

    lax.fori_loop(0, tm, issue, 0, unroll=4)
    pltpu.make_async_copy(ys_hbm.at[pl.ds(0, TOP_K_EXPERTS * tm)], buf, sem).wait()
    f = buf[0:tm]
    for k in range(1, TOP_K_EXPERTS):
        f = f + buf[k * tm:(k + 1) * tm]
    o_ref[...] = res_ref[...] + gate_ref[0] * f


def _moe_combine(ys, dest, res, gate, *, tm, tiles_per_seq):
    n, d = res.shape
    r = gate.shape[1]
    dest_km = dest.reshape(n, TOP_K_EXPERTS).T.reshape(-1)
    return pl.pallas_call(
        functools.partial(_combine_kernel, tm=tm),
        grid_spec=pltpu.PrefetchScalarGridSpec(
            num_scalar_prefetch=1,
            grid=(n // tm,),
            in_specs=[
                pl.BlockSpec(memory_space=pl.ANY),
                pl.BlockSpec((tm, d), lambda i, dr: (i, 0)),
                pl.BlockSpec((1, r, d), lambda i, dr: (i // tiles_per_seq, 0, 0)),
            ],
            out_specs=pl.BlockSpec((tm, d), lambda i, dr: (i, 0)),
            scratch_shapes=[pltpu.VMEM((TOP_K_EXPERTS * tm, d), jnp.float32), pltpu.SemaphoreType.DMA(())],
        ),
        out_shape=jax.ShapeDtypeStruct((n, d), jnp.float32),
        compiler_params=_params("arbitrary"),
        name="moe_combine",
    )(dest_km, ys, res, gate)


def _moe(hs, logits_list, wg, wu, wd, *, layer, tm, tf, res_list):
    h = jnp.concatenate(hs, axis=0) if len(hs) > 1 else hs[0]
    logits = jnp.concatenate(logits_list, axis=0) if len(hs) > 1 else logits_list[0]
    n, d = h.shape
    vals, idx = lax.top_k(logits[:, :N_EXPERTS], TOP_K_EXPERTS)
    gates = jax.nn.softmax(vals, axis=-1)
    npair = n * TOP_K_EXPERTS
    nt = npair // tm + N_EXPERTS
    flat_e = idx.reshape(-1).astype(jnp.int32)
    order = jnp.argsort(flat_e, stable=True)
    sorted_e = flat_e[order]
    counts = jnp.sum(flat_e[:, None] == jnp.arange(N_EXPERTS)[None, :], axis=0).astype(jnp.int32)
    tiles_e = (counts + tm - 1) // tm
    tile_end = jnp.cumsum(tiles_e)
    tile_start = tile_end - tiles_e
    grp_start = jnp.cumsum(counts) - counts
    rank = jnp.arange(npair, dtype=jnp.int32) - grp_start[sorted_e]
    dest_sorted = tile_start[sorted_e] * tm + rank
    dest = jnp.zeros((npair,), jnp.int32).at[order].set(dest_sorted)
    rows = nt * tm
    row_token = jnp.zeros((rows,), jnp.int32).at[dest].set(jnp.arange(npair, dtype=jnp.int32) // TOP_K_EXPERTS)
    row_gate = jnp.zeros((rows,), jnp.float32).at[dest].set(gates.reshape(-1))
    tile_ids = jnp.arange(nt, dtype=jnp.int32)
    total = tile_end[-1]
    tile_valid = (tile_ids < total).astype(jnp.int32)
    last_e = jnp.searchsorted(tile_end, total - 1, side="right").astype(jnp.int32)
    tile_expert = jnp.where(tile_valid > 0, jnp.searchsorted(tile_end, tile_ids, side="right").astype(jnp.int32),
                            last_e)
    tile_expert = jnp.clip(tile_expert, 0, N_EXPERTS - 1)
    ys = _ffn(h, wg, wu, wd, layer=layer, tm=tm, tf=tf, tile_expert=tile_expert, tile_valid=tile_valid,
              row_token=row_token, row_scale=row_gate[:, None])
    outs = []
    start = 0
    for hh, (resid, gate, tmc, tps) in zip(hs, res_list):
        stop = start + hh.shape[0] * TOP_K_EXPERTS
        outs.append(_moe_combine(ys, dest[start:stop], resid, gate, tm=tmc, tiles_per_seq=tps))
        start = stop
    return outs


INT_MIN = -(2 ** 31)
NEG = -1e30


def _sortable(score):
    bits = pltpu.bitcast(score, jnp.int32)
    return bits ^ ((bits >> 31) & jnp.int32(0x7FFFFFFF))


def _kth_largest(count_ge, shape, n_sel):
    def step(s, t):
        cand = t + lax.shift_left(jnp.int32(1), 31 - s)
        return jnp.where(count_ge(cand) >= n_sel, cand, t)

    return lax.fori_loop(0, 32, step, jnp.full(shape, INT_MIN, jnp.int32))


def _psat_kernel(q_ref, qi_ref, k_ref, vt_ref, tail_ref, wt_ref, o_ref, key_scr, m_scr, l_scr, acc_scr, *,
                 tq, tk, n_sel):
    i = pl.program_id(1)
    nck = (i + 1) * (tq // tk)
    row0 = i * tq
    nt = (((1,), (1,)), ((), ()))
    lane = lax.broadcasted_iota(jnp.int32, (tk, LANES), 1)
    kidx0 = lax.broadcasted_iota(jnp.int32, (tk, tq), 0)
    qidx = row0 + lax.broadcasted_iota(jnp.int32, (tk, tq), 1)

    def score_chunk(c, carry):
        ks = pl.multiple_of(c * tk, tk)
        kf = jnp.where(lane < HD_IDX, tail_ref[pl.ds(ks, tk), :], 0.0)
        k_even = kf.astype(jnp.bfloat16)
        k_odd = pltpu.roll(kf, HD_IDX, 1).astype(jnp.bfloat16)
        sc = jnp.zeros((tk, tq), jnp.float32)
        for h in range(H_IDX):
            qp = qi_ref[:, (h // 2) * LANES:(h // 2 + 1) * LANES]
            d = lax.dot_general(k_even if h % 2 == 0 else k_odd, qp, nt, preferred_element_type=jnp.float32)
            sc = sc + jnp.maximum(d, 0.0) * wt_ref[h:h + 1, :]
        key_scr[c] = jnp.where(ks + kidx0 <= qidx, _sortable(sc), INT_MIN)
        return carry

    lax.fori_loop(0, nck, score_chunk, 0)

    def count_ge(cand):
        def chunk(c, acc):
            hit = jnp.where(key_scr[c] >= cand, 1.0, 0.0)
            return acc + jnp.sum(hit.reshape(tk // 8, 8, tq), axis=0)

        acc = lax.fori_loop(0, nck, chunk, jnp.zeros((8, tq), jnp.float32))
        return jnp.sum(acc, axis=0, keepdims=True)

    thr = jnp.maximum(_kth_largest(count_ge, (1, tq), float(n_sel)), INT_MIN + 1)

    need = float(n_sel) - count_ge(thr + 1)
    tri = (lax.broadcasted_iota(jnp.int32, (tk, tk), 0) >= lax.broadcasted_iota(jnp.int32, (tk, tk), 1))
    tri = jnp.where(tri, 1.0, 0.0).astype(jnp.bfloat16)

    def select_chunk(c, ties_before):
        kk = key_scr[c]
        eq = kk == thr
        rank = jnp.dot(tri, jnp.where(eq, 1.0, 0.0).astype(jnp.bfloat16),
                       preferred_element_type=jnp.float32) + ties_before
        key_scr[c] = jnp.where((kk > thr) | (eq & (rank <= need)), 1, 0)
        return rank[tk - 1:tk, :]

    lax.fori_loop(0, nck, select_chunk, jnp.zeros((1, tq), jnp.float32))

    m_scr[...] = jnp.full_like(m_scr, NEG)
    l_scr[...] = jnp.zeros_like(l_scr)
    acc_scr[...] = jnp.zeros_like(acc_scr)
    c_exp = (HD_A ** -0.5) * math.log2(math.e)

    def att_chunk(c, carry):
        ks = pl.multiple_of(c * tk, tk)
        bias = jnp.where(key_scr[c] > 0, 0.0, NEG)
        bias = jnp.concatenate([bias] * G_A, axis=1)
        for g in range(KV_A):
            qg = jnp.concatenate(
                [q_ref[:, (g * G_A + j) * HD_A:(g * G_A + j + 1) * HD_A] for j in range(G_A)], axis=0)
            kg = k_ref[pl.ds(ks, tk), g * HD_A:(g + 1) * HD_A]
            s = lax.dot_general(kg, qg, nt, preferred_element_type=jnp.float32) + bias
            m_old = m_scr[g]
            m_new = jnp.maximum(m_old, jnp.max(s, axis=0, keepdims=True))
            alpha = jnp.exp2((m_old - m_new) * c_exp)
            p = jnp.exp2((s - m_new) * c_exp)
            l_scr[g] = alpha * l_scr[g] + jnp.sum(p, axis=0, keepdims=True)
            vtg = vt_ref[c, g * HD_A:(g + 1) * HD_A, :]
            acc_scr[g] = alpha * acc_scr[g] + jnp.dot(vtg, p.astype(jnp.bfloat16),
                                                      preferred_element_type=jnp.float32)
            m_scr[g] = m_new
        return carry

    lax.fori_loop(0, nck, att_chunk, 0)

    for g in range(KV_A):
        og = acc_scr[g] * (1.0 / l_scr[g])
        for j in range(G_A):
            h = g * G_A + j
            o_ref[:, h * HD_A:(h + 1) * HD_A] = og[:, j * tq:(j + 1) * tq].T.astype(o_ref.dtype)


def _prompt_sparse_attn_t(q, qi, k, v, tail, *, b, t, tq, tk, n_sel):
    nqt = t // tq
    nck = t // tk
    kvw = v.shape[1]
    vt = v.reshape(b, nck, tk, kvw).transpose(0, 1, 3, 2)
    wt = tail[:, HD_IDX:HD_IDX + H_IDX].T
    return pl.pallas_call(
        functools.partial(_psat_kernel, tq=tq, tk=tk, n_sel=n_sel),
        grid=(b, nqt),
        in_specs=[
            pl.BlockSpec((tq, q.shape[1]), lambda bb, i: (bb * nqt + i, 0)),
            pl.BlockSpec((tq, qi.shape[1]), lambda bb, i: (bb * nqt + i, 0)),
            pl.BlockSpec((t, k.shape[1]), lambda bb, i: (bb, 0)),
            pl.BlockSpec((None, nck, kvw, tk), lambda bb, i: (bb, 0, 0, 0)),
            pl.BlockSpec((t, LANES), lambda bb, i: (bb, 0)),
            pl.BlockSpec((H_IDX, tq), lambda bb, i: (0, bb * nqt + i)),
        ],
        out_specs=pl.BlockSpec((tq, q.shape[1]), lambda bb, i: (bb * nqt + i, 0)),
        out_shape=jax.ShapeDtypeStruct(q.shape, jnp.bfloat16),
        scratch_shapes=[
            pltpu.VMEM((nck, tk, tq), jnp.int32),
            pltpu.VMEM((KV_A, 1, G_A * tq), jnp.float32),
            pltpu.VMEM((KV_A, 1, G_A * tq), jnp.float32),
            pltpu.VMEM((KV_A, HD_A, G_A * tq), jnp.float32),
        ],
        compiler_params=_params("parallel", "arbitrary"),
        name="prompt_sparse_attn_t",
    )(q, qi, k, vt, tail, wt)


QPAD = 8
PAGES_PER_STEP = 16
PAGES_PER_INDEX_STEP = 32


def _ssi_kernel(pt_ref, qi_ref, w_ref, kn_ref, *refs, pp, n_sel, s_new, ns):
    del pt_ref
    page_refs = refs[:pp]
    bias_ref, bias_new_ref, key_scr = refs[pp:]
    s = pl.program_id(1)
    n = pp * PAGE_SIZE

    def scores(kt):
        d = jnp.dot(qi_ref[...], kt, preferred_element_type=jnp.float32)
        sc = jnp.maximum(d, 0.0) * w_ref[...]
        return jnp.sum(sc.reshape(QPAD, H_IDX, kt.shape[1]), axis=1)

    kt = jnp.concatenate([r[...] for r in page_refs], axis=1).astype(jnp.bfloat16)
    qrow = lax.broadcasted_iota(jnp.int32, (QPAD, n), 0)
    key_scr[s] = jnp.where(qrow < s_new, _sortable(scores(kt)), INT_MIN)

    @pl.when(s == ns - 1)
    def _():
        qr = lax.broadcasted_iota(jnp.int32, (QPAD, PAGE_SIZE), 0)
        col = lax.broadcasted_iota(jnp.int32, (QPAD, PAGE_SIZE), 1)
        key_new = jnp.where((col <= qr) & (qr < s_new), _sortable(scores(kn_ref[...])), INT_MIN)

        def count_ge(cand):
            acc = jnp.where(key_new >= cand, 1.0, 0.0)
            for c in range(ns):
                hit = jnp.where(key_scr[c] >= cand, 1.0, 0.0)
                for u in range(n // LANES):
                    acc = acc + hit[:, u * LANES:(u + 1) * LANES]
            return jnp.sum(acc, axis=1, keepdims=True)

        thr = jnp.maximum(_kth_largest(count_ge, (QPAD, 1), float(n_sel)), INT_MIN + 1)

        need = float(n_sel) - count_ge(thr + 1)
        nblk = n // LANES
        r_i = lax.broadcasted_iota(jnp.int32, (nblk * QPAD, nblk * QPAD), 0)
        c_i = lax.broadcasted_iota(jnp.int32, (nblk * QPAD, nblk * QPAD), 1)
        earlier = jnp.where((r_i % QPAD == c_i % QPAD) & (c_i // QPAD < r_i // QPAD), 1.0, 0.0).astype(jnp.bfloat16)
        upto = lax.broadcasted_iota(jnp.int32, (LANES, LANES), 0) <= lax.broadcasted_iota(jnp.int32, (LANES, LANES), 1)
        upto = jnp.where(upto, 1.0, 0.0).astype(jnp.bfloat16)

        def select(kk, ties_before, blocks):
            rep = lambda a: jnp.concatenate([a] * blocks, axis=0) if blocks > 1 else a
            eq = kk == rep(thr)
            rank = jnp.dot(jnp.where(eq, 1.0, 0.0).astype(jnp.bfloat16), upto, preferred_element_type=jnp.float32)
            if blocks > 1:
                rank = rank + jnp.dot(earlier, rank.astype(jnp.bfloat16),
                                      preferred_element_type=jnp.float32)[:, LANES - 1:LANES]
            rank = rank + rep(ties_before)
            sel = (kk > rep(thr)) | (eq & (rank <= rep(need)))
            return jnp.where(sel, 0.0, NEG), rank[(blocks - 1) * QPAD:, LANES - 1:LANES]

        ties = jnp.zeros((QPAD, 1), jnp.float32)
        for c in range(ns):
            kc = key_scr[c]
            stacked = jnp.concatenate([kc[:, u * LANES:(u + 1) * LANES] for u in range(nblk)], axis=0)
            bias_c, ties = select(stacked, ties, nblk)
            for u in range(nblk):
                bias_ref[:, c * n + u * LANES:c * n + (u + 1) * LANES] = bias_c[u * QPAD:(u + 1) * QPAD]
        bias_new_ref[...] = select(key_new, ties, 1)[0]


def _ssa_kernel(pt_ref, q_ref, bias_ref, bias_new_ref, kn_ref, vn_ref, *refs, pp, ns):
    del pt_ref
    k_refs = refs[:pp]
    v_refs = refs[pp:2 * pp]
    o_ref, m_scr, l_scr, acc_scr = refs[2 * pp:]
    s = pl.program_id(1)
    nt = (((1,), (1,)), ((), ()))
    scale = HD_A ** -0.5

    @pl.when(s == 0)
    def _():
        m_scr[...] = jnp.full_like(m_scr, NEG)
        l_scr[...] = jnp.zeros_like(l_scr)
        acc_scr[...] = jnp.zeros_like(acc_scr)

    rows_g = G_A * QPAD

    def update(kg, vg, bias):
        bias = jnp.concatenate([bias] * (KV_A * G_A), axis=0)
        sc = jnp.concatenate(
            [lax.dot_general(q_ref[g], kg[g], nt, preferred_element_type=jnp.float32) for g in range(KV_A)],
            axis=0) * scale + bias
        m_old = m_scr[...]
        m_new = jnp.maximum(m_old, jnp.max(sc, axis=1, keepdims=True))
        alpha = jnp.exp(m_old - m_new)
        p = jnp.exp(sc - m_new)
        l_scr[...] = alpha * l_scr[...] + jnp.sum(p, axis=1, keepdims=True)
        pb = p.astype(jnp.bfloat16)
        pv = jnp.concatenate(
            [jnp.dot(pb[g * rows_g:(g + 1) * rows_g], vg[g], preferred_element_type=jnp.float32)
             for g in range(KV_A)], axis=0)
        acc_scr[...] = alpha * acc_scr[...] + pv
        m_scr[...] = m_new

    def head_rows(page_refs, g):
        return jnp.concatenate([r[pl.ds(g, PAGE_SIZE, stride=KV_A), :] for r in page_refs],
                               axis=0).astype(jnp.bfloat16)

    update([head_rows(k_refs, g) for g in range(KV_A)], [head_rows(v_refs, g) for g in range(KV_A)],
           bias_ref[...])

    @pl.when(s == ns - 1)
    def _():
        kn, vn = kn_ref[...], vn_ref[...]
        update([kn[:, g * HD_A:(g + 1) * HD_A] for g in range(KV_A)],
               [vn[:, g * HD_A:(g + 1) * HD_A] for g in range(KV_A)], bias_new_ref[...])
        o_ref[...] = acc_scr[...] / l_scr[...]


def _sample_sparse_attn(q, qi, wi, ki, k, v, cache_k, cache_v, cache_kidx, page_table, *, layer, n_sel):
    db, s_new = q.shape[:2]
    n_pages = page_table.shape[1]
    ppi, ppa = min(PAGES_PER_INDEX_STEP, n_pages), min(PAGES_PER_STEP, n_pages)
    nsi, nsa = n_pages // ppi, n_pages // ppa
    kvw = KV_A * HD_A
    rows = KV_A * G_A * QPAD
    pad_q = ((0, 0), (0, QPAD - s_new), (0, 0), (0, 0))
    pad_k = ((0, 0), (0, PAGE_SIZE - s_new), (0, 0))
    qi_r = jnp.pad(qi.reshape(db, s_new, H_IDX, HD_IDX), pad_q).reshape(db, QPAD * H_IDX, HD_IDX)
    w_r = jnp.pad(wi, ((0, 0), (0, QPAD - s_new), (0, 0))).reshape(db, QPAD * H_IDX, 1)
    ki_n = jnp.swapaxes(jnp.pad(ki.astype(jnp.bfloat16), pad_k), 1, 2)
    k_n = jnp.pad(k, pad_k)
    v_n = jnp.pad(v, pad_k)
    q_r = jnp.pad(q.reshape(db, s_new, KV_A, G_A, HD_A).transpose(0, 2, 3, 1, 4),
                  ((0, 0), (0, 0), (0, 0), (0, QPAD - s_new), (0, 0))).reshape(db, KV_A, G_A * QPAD, HD_A)

    def page_spec(rows_, pp, j):
        return pl.BlockSpec((None, None, rows_, LANES), lambda b, s, pt: (layer, pt[b, s * pp + j], 0, 0))

    cki = jnp.swapaxes(cache_kidx, 2, 3)
    ck = cache_k.reshape(cache_k.shape[:2] + (PAGE_SIZE * KV_A, HD_A))
    cv = cache_v.reshape(cache_v.shape[:2] + (PAGE_SIZE * KV_A, HD_A))

    bias, bias_new = pl.pallas_call(
        functools.partial(_ssi_kernel, pp=ppi, n_sel=n_sel, s_new=s_new, ns=nsi),
        grid_spec=pltpu.PrefetchScalarGridSpec(
            num_scalar_prefetch=1,
            grid=(db, nsi),
            in_specs=[
                pl.BlockSpec((None, QPAD * H_IDX, HD_IDX), lambda b, s, pt: (b, 0, 0)),
                pl.BlockSpec((None, QPAD * H_IDX, 1), lambda b, s, pt: (b, 0, 0)),
                pl.BlockSpec((None, HD_IDX, PAGE_SIZE), lambda b, s, pt: (b, 0, 0)),
            ] + [page_spec(HD_IDX, ppi, j) for j in range(ppi)],
            out_specs=[
                pl.BlockSpec((None, QPAD, n_pages * PAGE_SIZE), lambda b, s, pt: (b, 0, 0)),
                pl.BlockSpec((None, QPAD, PAGE_SIZE), lambda b, s, pt: (b, 0, 0)),
            ],
            scratch_shapes=[pltpu.VMEM((nsi, QPAD, ppi * PAGE_SIZE), jnp.int32)],
        ),
        out_shape=[jax.ShapeDtypeStruct((db, QPAD, n_pages * PAGE_SIZE), jnp.float32),
                   jax.ShapeDtypeStruct((db, QPAD, PAGE_SIZE), jnp.float32)],
        compiler_params=_params("parallel", "arbitrary"),
        name="sample_indexer",
    )(page_table, qi_r, w_r, ki_n, *([cki] * ppi))

    o = pl.pallas_call(
        functools.partial(_ssa_kernel, pp=ppa, ns=nsa),
        grid_spec=pltpu.PrefetchScalarGridSpec(
            num_scalar_prefetch=1,
            grid=(db, nsa),
            in_specs=[
                pl.BlockSpec((None, KV_A, G_A * QPAD, HD_A), lambda b, s, pt: (b, 0, 0, 0)),
                pl.BlockSpec((None, QPAD, ppa * PAGE_SIZE), lambda b, s, pt: (b, 0, s)),
                pl.BlockSpec((None, QPAD, PAGE_SIZE), lambda b, s, pt: (b, 0, 0)),
                pl.BlockSpec((None, PAGE_SIZE, kvw), lambda b, s, pt: (b, 0, 0)),
                pl.BlockSpec((None, PAGE_SIZE, kvw), lambda b, s, pt: (b, 0, 0)),
            ] + [page_spec(ck.shape[2], ppa, j) for j in range(ppa)] * 2,
            out_specs=pl.BlockSpec((None, rows, HD_A), lambda b, s, pt: (b, 0, 0)),
            scratch_shapes=[
                pltpu.VMEM((rows, 1), jnp.float32),
                pltpu.VMEM((rows, 1), jnp.float32),
                pltpu.VMEM((rows, HD_A), jnp.float32),
            ],
        ),
        out_shape=jax.ShapeDtypeStruct((db, rows, HD_A), jnp.float32),
        compiler_params=_params("parallel", "arbitrary"),
        name="sample_sparse_attn",
    )(page_table, q_r, bias, bias_new, k_n, v_n, *([ck] * ppa), *([cv] * ppa))
    o = o.reshape(db, KV_A, G_A, QPAD, HD_A)[:, :, :, :s_new].transpose(0, 3, 1, 2, 4)
    return o.reshape(db * s_new, H_A * HD_A).astype(jnp.bfloat16)


def _wa_kernel(q_ref, kp_ref, kc_ref, vp_ref, vc_ref, sink_ref, o_ref, *, tq, blocks_per_seq, mask_first):
    n = pl.program_id(0)
    nt = (((1,), (1,)), ((), ()))
    r = lax.broadcasted_iota(jnp.int32, (tq, 2 * WINDOW), 0)
    c = lax.broadcasted_iota(jnp.int32, (tq, 2 * WINDOW), 1)
    ok = (c - r >= 1) & (c - r <= WINDOW)
    if mask_first:
        ok = ok & ((n % blocks_per_seq != 0) | (c >= WINDOW))
    bias = jnp.where(ok, 0.0, NEG)
    bias = jnp.concatenate([bias] * G_B, axis=0)
    scale = HD_B ** -0.5
    for g in range(KV_B):
        cols = slice(g * HD_B, (g + 1) * HD_B)
        kcat = jnp.concatenate([kp_ref[:, cols], kc_ref[:, cols]], axis=0)
        vcat = jnp.concatenate([vp_ref[:, cols], vc_ref[:, cols]], axis=0)
        qg = jnp.concatenate(
            [q_ref[:, (g * G_B + j) * HD_B:(g * G_B + j + 1) * HD_B] for j in range(G_B)], axis=0)
        s = lax.dot_general(qg, kcat, nt, preferred_element_type=jnp.float32) * scale + bias
        sink = sink_ref[g]
        m = jnp.maximum(jnp.max(s, axis=1, keepdims=True), sink)
        p = jnp.exp(s - m)
        p = p / (jnp.sum(p, axis=1, keepdims=True) + jnp.exp(sink - m))
        og = jnp.dot(p.astype(jnp.bfloat16), vcat, preferred_element_type=jnp.float32)
        for j in range(G_B):
            h = g * G_B + j
            o_ref[:, h * HD_B:(h + 1) * HD_B] = og[j * tq:(j + 1) * tq].astype(o_ref.dtype)


def _wat_kernel(q_ref, kp_ref, kc_ref, vtp_ref, vtc_ref, sink_ref, o_ref, *, blocks_per_seq):
    tq = WINDOW
    n = pl.program_id(0)
    nt = (((1,), (1,)), ((), ()))
    kidx = lax.broadcasted_iota(jnp.int32, (2 * WINDOW, tq), 0)
    qidx = lax.broadcasted_iota(jnp.int32, (2 * WINDOW, tq), 1)
    ok = (kidx - qidx >= 1) & (kidx - qidx <= WINDOW) & ((n % blocks_per_seq != 0) | (kidx >= WINDOW))
    bias = jnp.where(ok, 0.0, NEG)
    bias = jnp.concatenate([bias] * G_B, axis=1)
    scale = HD_B ** -0.5
    for g in range(KV_B):
        cols = slice(g * HD_B, (g + 1) * HD_B)
        kcat = jnp.concatenate([kp_ref[:, cols], kc_ref[:, cols]], axis=0)
        vcat_t = jnp.concatenate([vtp_ref[cols, :], vtc_ref[cols, :]], axis=1)
        qg = jnp.concatenate(
            [q_ref[:, (g * G_B + j) * HD_B:(g * G_B + j + 1) * HD_B] for j in range(G_B)], axis=0)
        s = lax.dot_general(kcat, qg, nt, preferred_element_type=jnp.float32) * scale + bias
        sink = sink_ref[g]
        m = jnp.maximum(jnp.max(s, axis=0, keepdims=True), sink)
        p = jnp.exp(s - m)
        inv = 1.0 / (jnp.sum(p, axis=0, keepdims=True) + jnp.exp(sink - m))
        ot = jnp.dot(vcat_t, (p * inv).astype(jnp.bfloat16), preferred_element_type=jnp.float32)
        for j in range(0, G_B, 2):
            pair = jnp.concatenate([ot[:, j * tq:(j + 1) * tq], ot[:, (j + 1) * tq:(j + 2) * tq]], axis=0)
            h = g * G_B + j
            o_ref[:, h * HD_B:(h + 2) * HD_B] = pair.T.astype(o_ref.dtype)


def _window_attn_t(q, k, v_t, sinks, *, blocks_per_seq):
    tq = WINDOW
    nblk = q.shape[0] // tq
    kvw = KV_B * HD_B
    sink_rows = jnp.repeat(sinks.astype(jnp.float32).reshape(KV_B, G_B), tq, axis=1)[:, None, :]
    return pl.pallas_call(
        functools.partial(_wat_kernel, blocks_per_seq=blocks_per_seq),
        grid=(nblk,),
        in_specs=[
            pl.BlockSpec((tq, q.shape[1]), lambda n: (n, 0)),
            pl.BlockSpec((WINDOW, kvw), lambda n: (jnp.maximum(n - 1, 0), 0)),
            pl.BlockSpec((WINDOW, kvw), lambda n: (n, 0)),
            pl.BlockSpec((kvw, WINDOW), lambda n: (0, jnp.maximum(n - 1, 0))),
            pl.BlockSpec((kvw, WINDOW), lambda n: (0, n)),
            pl.BlockSpec((KV_B, 1, G_B * tq), lambda n: (0, 0, 0)),
        ],
        out_specs=pl.BlockSpec((tq, q.shape[1]), lambda n: (n, 0)),
        out_shape=jax.ShapeDtypeStruct(q.shape, jnp.bfloat16),
        compiler_params=_params("parallel"),
        name="window_attn_t",
    )(q, k, k, v_t, v_t, sink_rows)


def _window_attn(q, k_prev, k_cur, v_prev, v_cur, sinks, *, tq, blocks_per_seq, mask_first):
    nblk = q.shape[0] // tq
    kvw = KV_B * HD_B
    if mask_first:
        prev = lambda n: (jnp.maximum(n - 1, 0), 0)
    else:
        prev = lambda n: (n, 0)
    cur = lambda n: (n, 0)
    sink_rows = jnp.repeat(sinks.astype(jnp.float32).reshape(KV_B, G_B), tq, axis=1)[:, :, None]
    return pl.pallas_call(
        functools.partial(_wa_kernel, tq=tq, blocks_per_seq=blocks_per_seq, mask_first=mask_first),
        grid=(nblk,),
        in_specs=[
            pl.BlockSpec((tq, q.shape[1]), cur),
            pl.BlockSpec((WINDOW, kvw), prev),
            pl.BlockSpec((WINDOW, kvw), cur),
            pl.BlockSpec((WINDOW, kvw), prev),
            pl.BlockSpec((WINDOW, kvw), cur),
            pl.BlockSpec((KV_B, G_B * tq, 1), lambda n: (0, 0, 0)),
        ],
        out_specs=pl.BlockSpec((tq, q.shape[1]), cur),
        out_shape=jax.ShapeDtypeStruct(q.shape, jnp.bfloat16),
        compiler_params=_params("parallel"),
        name="window_attn",
    )(q, k_prev, k_cur, v_prev, v_cur, sink_rows)


def _trunk(x, c_mod, kv_mod, f_mod, pos, weights, *, tm, tm_mm, sparse_attn, make_ctx):
    (g_norm, w_in_a, w_out_a, g_kv, w_kv_b, w_q_b, sinks_b, w_out_b, w_gate_d, w_up_d, w_down_d,
     w_router, b_router, w_gate_e, w_up_e, w_down_e, g_final) = weights
    b, t, d = x.shape
    m = b * t
    per_row = tm > t
    tps = 1 if per_row else t // tm
    tps_mm = 1 if per_row else t // tm_mm

    def seq(v):
        if per_row:
            return jnp.repeat(v, t, axis=0)[None]
        return v[:, None, :]

    pos_rows = jnp.tile(pos, b) if per_row else pos
    tab_a = _rope_table(pos_rows, HD_A, ROT_A)
    tab_i = _rope_table(pos_rows, HD_IDX, ROT_IDX)
    tab_t = _tail_table(pos_rows)
    tab_b = _rope_table(pos_rows, HD_B, ROT_B)

    xf = x.reshape(m, d)
    rows_a = []
    ctx = None
    for l in range(DEPTH):
        if l == N_A_LAYERS:
            shift, scale = jnp.split(kv_mod, 2, axis=-1)
            h = _adaln(xf, g_kv, seq(shift), seq(scale), tm=tm, tiles_per_seq=tps)
            kvw_b = KV_B * HD_B
            kb, kb_bf = _matmul(h, w_kv_b, col0=0, ncols=kvw_b, tm=tm_mm, tn=kvw_b, rope=(tab_b, ROT_B // 2),
                                out_dtypes=(jnp.float32, jnp.bfloat16))
            vb, vb_bf = _matmul(h, w_kv_b, col0=kvw_b, ncols=kvw_b, tm=tm_mm, tn=kvw_b,
                                out_dtypes=(jnp.float32, jnp.bfloat16))
            ctx = make_ctx(kb.reshape(b, t, KV_B, HD_B), vb.reshape(b, t, KV_B, HD_B), kb_bf, vb_bf)
        sh1, sc1, g1, sh2, sc2, g2 = jnp.split(c_mod[l], 6, axis=-1)
        h = _adaln(xf, g_norm[l, 0], seq(sh1), seq(sc1), tm=tm, tiles_per_seq=tps)
        if l < N_A_LAYERS:
            both = (jnp.float32, jnp.bfloat16)
            only_bf = (jnp.bfloat16,)
            kvw = KV_A * HD_A
            q = _matmul(h, w_in_a, layer=l, col0=0, ncols=QW_A, tm=tm_mm, tn=512, rope=(tab_a, ROT_A // 2),
                        out_dtypes=only_bf)
            k, k_bf = _matmul(h, w_in_a, layer=l, col0=QW_A, ncols=kvw, tm=tm_mm, tn=kvw,
                              rope=(tab_a, ROT_A // 2), out_dtypes=both)
            v, v_bf = _matmul(h, w_in_a, layer=l, col0=QW_A + kvw, ncols=kvw, tm=tm_mm, tn=kvw, out_dtypes=both)
            qi = _matmul(h, w_in_a, layer=l, col0=QW_A + 2 * kvw, ncols=H_IDX * HD_IDX, tm=tm_mm, tn=512,
                         rope=(tab_i, ROT_IDX // 2), out_dtypes=only_bf)
            w_tail = jnp.pad(w_in_a[l][:, QW_A + 2 * kvw + H_IDX * HD_IDX:],
                             ((0, 0), (0, LANES - HD_IDX - H_IDX)))
            tail = _matmul(h, w_tail, tm=tm_mm, tn=LANES, rope=(tab_t, ROT_IDX // 2))
            o = sparse_attn(l, q, qi, k_bf, v_bf, tail)
            rows_a.append((k.reshape(b, t, KV_A, HD_A), v.reshape(b, t, KV_A, HD_A),
                           tail[:, :HD_IDX].reshape(b, t, HD_IDX)))
            w_o, lo = w_out_a, l
        else:
            lb = l - N_A_LAYERS
            q = _matmul(h, w_q_b, layer=lb, tm=tm_mm, tn=512, rope=(tab_b, ROT_B // 2),
                        out_dtypes=(jnp.bfloat16,))
            o = ctx[0](q, sinks_b[lb])
            w_o, lo = w_out_b, lb
        xf = _matmul(o, w_o, layer=lo, tm=tm_mm, tn=512, res=(xf, seq(g1), tps_mm))
        if l % 2 == 0:
            h = _adaln(xf, g_norm[l, 1], seq(sh2), seq(sc2), tm=tm, tiles_per_seq=tps)
            xf = _ffn(h, w_gate_d, w_up_d, w_down_d, layer=l // 2, tm=tm, tf=256, res=(xf, seq(g2), tps))
        else:
            h, logits = _adaln(xf, g_norm[l, 1], seq(sh2), seq(sc2), tm=tm, tiles_per_seq=tps,
                               out_dtype=jnp.float32, router=(w_router[l // 2], b_router[l // 2]))
            tmc = min(tm, 256)
            xf = yield (l // 2, h, logits, (xf, seq(g2), tmc, 1 if per_row else t // tmc))
    fsh, fsc = jnp.split(f_mod, 2, axis=-1)
    y = _adaln(xf, g_final, seq(fsh), seq(fsc), tm=tm, tiles_per_seq=tps, out_dtype=jnp.float32)
    return y.reshape(b, t, d), rows_a, ctx


def kernel(x_prompt, x_sample, cache_k_a, cache_v_a, cache_kidx_a, state_k_b, state_v_b, page_table,
           c_prompt, c_sample, g_norm, w_mod, b_mod, w_in_a, w_out_a, g_kv, w_kvmod, b_kvmod, w_kv_b,
           w_q_b, sinks_b, w_out_b, w_gate_d, w_up_d, w_down_d, w_router, b_router, w_gate_e, w_up_e,
           w_down_e, g_final, w_fmod, b_fmod):
    n_pages = page_table.shape[1]
    past = n_pages * PAGE_SIZE
    nb_p, nb_s = c_prompt.shape[0], c_sample.shape[0]

    c_rows = 48
    c_all = jnp.zeros((c_rows, D_MODEL), jnp.bfloat16)
    c_all = c_all.at[:nb_p + nb_s].set(jnp.concatenate([c_prompt, c_sample], axis=0).astype(jnp.bfloat16))
    mods = [_matmul(c_all, w_mod, layer=l, tm=c_rows, tn=512, bias=b_mod[l][None]) for l in range(DEPTH)]
    kvm = _matmul(c_all, w_kvmod, tm=c_rows, tn=512, bias=b_kvmod[None])
    fm = _matmul(c_all, w_fmod, tm=c_rows, tn=512, bias=b_fmod[None])

    sb, st = x_sample.shape[:2]
    w_buf = state_k_b.shape[1]
    kvw_b = KV_B * HD_B

    def prompt_sparse_attn(li, q, qi, k_bf, v_bf, tail):
        b, t = x_prompt.shape[:2]
        return _prompt_sparse_attn_t(q, qi, k_bf, v_bf, tail, b=b, t=t, tq=256, tk=256,
                                   n_sel=min(TOPK_MAX, t // 4))

    def sample_sparse_attn(li, q, qi, k_bf, v_bf, tail):
        r3 = lambda a: a.reshape(sb, st, a.shape[-1])
        return _sample_sparse_attn(
            r3(q), r3(qi), r3(tail[:, HD_IDX:HD_IDX + H_IDX]), r3(tail[:, :HD_IDX]), r3(k_bf), r3(v_bf),
            cache_k_a, cache_v_a, cache_kidx_a, page_table, layer=li, n_sel=min(TOPK_MAX, (past + st) // 4))

    def prompt_ctx(kb, vb, kb_bf, vb_bf):
        t = kb.shape[1]
        wp = min(WINDOW, t)

        vb_t = vb_bf.T

        def attend(q, sinks):
            return _window_attn_t(q, kb_bf, vb_t, sinks, blocks_per_seq=t // WINDOW)

        return attend, kb[:, t - wp:], vb[:, t - wp:]

    def sample_ctx(kb, vb, kb_bf, vb_bf):
        tq = 16
        pad_new = lambda a: jnp.pad(a.reshape(sb, st, kvw_b), ((0, 0), (0, WINDOW - st), (0, 0))).reshape(-1, kvw_b)
        k_prev = state_k_b.astype(jnp.bfloat16).reshape(sb * w_buf, kvw_b)
        v_prev = state_v_b.astype(jnp.bfloat16).reshape(sb * w_buf, kvw_b)
        k_cur, v_cur = pad_new(kb_bf), pad_new(vb_bf)

        def attend(q, sinks):
            qp = jnp.pad(q.reshape(sb, st, -1), ((0, 0), (0, tq - st), (0, 0))).reshape(sb * tq, -1)
            o = _window_attn(qp, k_prev, k_cur, v_prev, v_cur, sinks, tq=tq, blocks_per_seq=1, mask_first=False)
            return o.reshape(sb, tq, -1)[:, :st].reshape(sb * st, -1)

        k_ctx = jnp.concatenate([state_k_b, kb], axis=1)
        v_ctx = jnp.concatenate([state_v_b, vb], axis=1)
        return attend, k_ctx[:, -w_buf:], v_ctx[:, -w_buf:]

    weights = (g_norm, w_in_a, w_out_a, g_kv, w_kv_b, w_q_b, sinks_b, w_out_b, w_gate_d, w_up_d, w_down_d,
               w_router, b_router, w_gate_e, w_up_e, w_down_e, g_final)
    pos_p = jnp.arange(x_prompt.shape[1])
    pos_s = past + jnp.arange(x_sample.shape[1])
    trunks = [
        _trunk(x_prompt, [mm[:nb_p] for mm in mods], kvm[:nb_p], fm[:nb_p], pos_p, weights,
               tm=1024, tm_mm=2048, sparse_attn=prompt_sparse_attn, make_ctx=prompt_ctx),
        _trunk(x_sample, [mm[nb_p:nb_p + nb_s] for mm in mods], kvm[nb_p:nb_p + nb_s], fm[nb_p:nb_p + nb_s], pos_s,
               weights, tm=sb * st, tm_mm=sb * st, sparse_attn=sample_sparse_attn, make_ctx=sample_ctx),
    ]
    requests = [next(tr) for tr in trunks]
    results = [None] * len(trunks)
    while any(r is None for r in results):
        layer = requests[0][0]
        new_x = _moe([r[1] for r in requests], [r[2] for r in requests], w_gate_e, w_up_e, w_down_e,
                     layer=layer, tm=MOE_ROW_TILE, tf=256, res_list=[r[3] for r in requests])
        for n, (tr, xn) in enumerate(zip(trunks, new_x)):
            try:
                requests[n] = tr.send(xn)
            except StopIteration as done:
                results[n] = done.value
    (y_prompt, rows_p, ctx_p), (y_sample, rows_s, ctx_s) = results

    return (y_prompt, y_sample,
            jnp.stack([r[0] for r in rows_p]), jnp.stack([r[1] for r in rows_p]), jnp.stack([r[2] for r in rows_p]),
            jnp.stack([r[0] for r in rows_s]), jnp.stack([r[1] for r in rows_s]), jnp.stack([r[2] for r in rows_s]),
            ctx_p[1], ctx_p[2], ctx_s[1], ctx_s[2])
```
